```python
import jax
import jax.numpy as jnp
from jax import lax
import numpy as np

D_MODEL = 1024
BATCH = 8
SEQ = 4096
DEPTH = 2

HEAD_DIM = 64
SB_HEADS = D_MODEL // 256
SB_WIDTH = SB_HEADS * HEAD_DIM
CV_WIDTH = D_MODEL // 4
CV_KERNEL = 31
DL_HEADS = D_MODEL // 128
DL_WIDTH = DL_HEADS * HEAD_DIM
MIX_WIDTH = SB_WIDTH + CV_WIDTH + DL_WIDTH
IN_WIDTH = 3 * SB_WIDTH + 2 * CV_WIDTH + 3 * DL_WIDTH
DL_PATTERN = ((128, 1), (512, 4), (2048, 16))
BLOCK = 128
ROPE_THETA = 10000.0
N_MEM = 256
X_HEADS = 4
X_HEAD_DIM = D_MODEL // X_HEADS
D_FF = 2816
FFN_KERNEL = 3
EPS = 1e-6

kernel_name = 'hybrid_stickbreak_conformer_dilated'


def rms_norm(x, g):
    xf = x.astype(jnp.float32)
    y = xf * lax.rsqrt(jnp.mean(xf * xf, axis=-1, keepdims=True) + EPS)
    return (y * g.astype(jnp.float32)).astype(x.dtype)


def layer_norm(x, g, b):
    xf = x.astype(jnp.float32)
    mu = jnp.mean(xf, axis=-1, keepdims=True)
    var = jnp.mean(jnp.square(xf - mu), axis=-1, keepdims=True)
    y = (xf - mu) * lax.rsqrt(var + EPS)
    return (y * g.astype(jnp.float32) + b.astype(jnp.float32)).astype(x.dtype)


def causal_depthwise_conv(x, w, b):
    k_width, ch = w.shape
    y = lax.conv_general_dilated(
        x, w[:, None, :].astype(x.dtype), window_strides=(1,),
        padding=((k_width - 1, 0),), dimension_numbers=('NWC', 'WIO', 'NWC'),
        feature_group_count=ch)
    return y + b.astype(x.dtype)


def rope(x, positions):
    half = x.shape[-1] // 2
    inv_freq = ROPE_THETA ** (-jnp.arange(half, dtype=jnp.float32) / half)
    ang = positions.astype(jnp.float32)[..., None] * inv_freq
    cos = jnp.cos(ang)[:, :, None, :]
    sin = jnp.sin(ang)[:, :, None, :]
    xf = x.astype(jnp.float32)
    x1, x2 = xf[..., :half], xf[..., half:]
    return jnp.concatenate([x1 * cos - x2 * sin, x2 * cos + x1 * sin], axis=-1).astype(x.dtype)


def stick_breaking_attention(q, k, v):
    b_, s_, h_, hd = q.shape
    scale = hd ** -0.5
    qh, kh, vh = (t.transpose(0, 2, 1, 3) for t in (q, k, v))
    outs = []
    for n in range(s_ // BLOCK):
        t0, t1 = n * BLOCK, (n + 1) * BLOCK
        z = jnp.einsum('bhqc,bhkc->bhqk', qh[:, :, t0:t1], kh[:, :, :t1]).astype(jnp.float32) * scale
        t_idx = t0 + jnp.arange(BLOCK)[:, None]
        s_idx = jnp.arange(t1)[None, :]
        before = s_idx < t_idx
        log_keep = jnp.where(before, jax.nn.log_sigmoid(-z), 0.0)
        between = lax.cumsum(log_keep, axis=3, reverse=True) - log_keep
        a = jnp.where(before, jnp.exp(jax.nn.log_sigmoid(z) + between), 0.0)
        outs.append(jnp.einsum('bhqk,bhkc->bhqc', a.astype(vh.dtype), vh[:, :, :t1]))
    o = jnp.concatenate(outs, axis=2)
    return o.transpose(0, 2, 1, 3)


def dilated_branch(q, k, v, window, dilation):
    b_, s_, h_, hd = q.shape
    w_steps = window // dilation
    assert w_steps <= BLOCK
    seq_sub = s_ // dilation
    nb = -(-seq_sub // BLOCK)
    pad_len = nb * BLOCK - seq_sub
    scale = hd ** -0.5

    def to_sub(t):
        t = t.reshape(b_, seq_sub, dilation, h_, hd).transpose(0, 2, 3, 1, 4)
        t = jnp.pad(t, ((0, 0), (0, 0), (0, 0), (0, pad_len), (0, 0)))
        return t.reshape(b_, dilation, h_, nb, BLOCK, hd)

    def with_prev(t):
        prev = jnp.pad(t, ((0, 0), (0, 0), (0, 0), (1, 0), (0, 0), (0, 0)))[:, :, :, :-1]
        return jnp.concatenate([prev, t], axis=4)

    qs = to_sub(q)
    kw = with_prev(to_sub(k))
    vw = with_prev(to_sub(v))
    s = jnp.einsum('brhnqc,brhnkc->brhnqk', qs, kw).astype(jnp.float32) * scale
    qi = jnp.arange(nb)[:, None, None] * BLOCK + jnp.arange(BLOCK)[None, :, None]
    ki = (jnp.arange(nb)[:, None, None] - 1) * BLOCK + jnp.arange(2 * BLOCK)[None, None, :]
    dist = qi - ki
    valid = (dist >= 0) & (dist <= w_steps) & (ki >= 0)
    s = jnp.where(valid, s, -jnp.inf)
    m = jnp.max(s, axis=-1, keepdims=True)
    p = jnp.exp(s - m)
    den = jnp.sum(p, axis=-1, keepdims=True)
    o = jnp.einsum('brhnqk,brhnkc->brhnqc', (p / den).astype(v.dtype), vw)
    lse = (m + jnp.log(den))[..., 0]
    o = o.reshape(b_, dilation, h_, nb * BLOCK, hd)[:, :, :, :seq_sub]
    o = o.transpose(0, 3, 1, 2, 4).reshape(b_, s_, h_, hd)
    lse = lse.reshape(b_, dilation, h_, nb * BLOCK)[:, :, :, :seq_sub]
    lse = lse.transpose(0, 3, 1, 2).reshape(b_, s_, h_)
    return o, lse


def dilated_mixture(q, k, v):
    outs, lses = [], []
    for window, dilation in DL_PATTERN:
        o, lse = dilated_branch(q, k, v, window, dilation)
        outs.append(o)
        lses.append(lse)
    wts = jax.nn.softmax(jnp.stack(lses, axis=-1), axis=-1)
    o = jnp.einsum('bshn,nbshc->bshc', wts, jnp.stack(outs, axis=0).astype(jnp.float32))
    return o.astype(v.dtype)


def conformer_conv(val, gate, cv_w, cv_b, cv_ln_g, cv_ln_b, cv_pw_w, cv_pw_b):
    g = val * jax.nn.sigmoid(gate)
    c = causal_depthwise_conv(g, cv_w, cv_b)
    c = jax.nn.silu(layer_norm(c, cv_ln_g, cv_ln_b))
    return c @ cv_pw_w + cv_pw_b


def hybrid_mixer(h, positions, w_in, cv_w, cv_b, cv_ln_g, cv_ln_b, cv_pw_w, cv_pw_b, w_out):
    b_, s_, _ = h.shape
    u = h @ w_in
    o1 = 3 * SB_WIDTH
    o2 = o1 + 2 * CV_WIDTH
    sb = u[..., :o1].reshape(b_, s_, 3, SB_HEADS, HEAD_DIM)
    cv = u[..., o1:o2]
    dl = u[..., o2:].reshape(b_, s_, 3, DL_HEADS, HEAD_DIM)
    a_out = stick_breaking_attention(sb[:, :, 0], sb[:, :, 1], sb[:, :, 2]).reshape(b_, s_, SB_WIDTH)
    b_out = conformer_conv(cv[..., :CV_WIDTH], cv[..., CV_WIDTH:], cv_w, cv_b, cv_ln_g, cv_ln_b, cv_pw_w, cv_pw_b)
    q = rope(dl[:, :, 0], positions)
    k = rope(dl[:, :, 1], positions)
    c_out = dilated_mixture(q, k, dl[:, :, 2]).reshape(b_, s_, DL_WIDTH)
    return jnp.concatenate([a_out, b_out, c_out], axis=-1) @ w_out


def memory_cross_attention(h, mem_n, wq, wk, wv, wo):
    b_, s_, _ = h.shape
    q = (h @ wq).reshape(b_, s_, X_HEADS, X_HEAD_DIM)
    k = (mem_n @ wk).reshape(b_, -1, X_HEADS, X_HEAD_DIM)
    v = (mem_n @ wv).reshape(b_, -1, X_HEADS, X_HEAD_DIM)
    s = jnp.einsum('bshc,bmhc->bhsm', q, k).astype(jnp.float32) * (X_HEAD_DIM ** -0.5)
    p = jax.nn.softmax(s, axis=-1).astype(v.dtype)
    o = jnp.einsum('bhsm,bmhc->bshc', p, v).reshape(b_, s_, D_MODEL)
    return o @ wo


def conv_ffn(h, w_up, conv_w, conv_b, w_down):
    u = causal_depthwise_conv(h @ w_up, conv_w, conv_b)
    gate, val = u[..., :D_FF], u[..., D_FF:]
    return (jax.nn.gelu(gate, approximate=True) * val) @ w_down


def setup_inputs(seed: int = 0) -> dict:
    key = jax.random.key(seed)
    ks = jax.random.split(key, 32)
    f32 = jnp.float32

    def dense(k, shape, fan_in):
        return jax.random.normal(k, shape, f32) * (fan_in ** -0.5)

    def gain(k, shape):
        return 1.0 + 0.02 * jax.random.normal(k, shape, f32)

    def bias(k, shape):
        return 0.02 * jax.random.normal(k, shape, f32)

    offset = jax.random.randint(ks[2], (BATCH, 1), 0, 1024, dtype=jnp.int32)
    positions = (offset + jnp.arange(SEQ, dtype=jnp.int32)[None, :]).astype(jnp.int32)
    return {
        'x': jax.random.normal(ks[0], (BATCH, SEQ, D_MODEL), f32),
        'mem': jax.random.normal(ks[1], (BATCH, N_MEM, D_MODEL), f32),
        'positions': positions,
        'mix_norm_pre': gain(ks[3], (DEPTH, D_MODEL)),
        'w_in': dense(ks[4], (DEPTH, D_MODEL, IN_WIDTH), D_MODEL),
        'cv_w': dense(ks[5], (DEPTH, CV_KERNEL, CV_WIDTH), CV_KERNEL),
        'cv_b': bias(ks[6], (DEPTH, CV_WIDTH)),
        'cv_ln_g': gain(ks[7], (DEPTH, CV_WIDTH)),
        'cv_ln_b': bias(ks[8], (DEPTH, CV_WIDTH)),
        'cv_pw_w': dense(ks[9], (DEPTH, CV_WIDTH, CV_WIDTH), CV_WIDTH),
        'cv_pw_b': bias(ks[10], (DEPTH, CV_WIDTH)),
        'w_out': dense(ks[11], (DEPTH, MIX_WIDTH, D_MODEL), MIX_WIDTH),
        'mix_norm_post': gain(ks[12], (DEPTH, D_MODEL)),
        'x_norm_pre': gain(ks[13], (DEPTH, D_MODEL)),
        'mem_norm': gain(ks[14], (DEPTH, D_MODEL)),
        'x_wq': dense(ks[15], (DEPTH, D_MODEL, D_MODEL), D_MODEL),
        'x_wk': dense(ks[16], (DEPTH, D_MODEL, D_MODEL), D_MODEL),
        'x_wv': dense(ks[17], (DEPTH, D_MODEL, D_MODEL), D_MODEL),
        'x_wo': dense(ks[18], (DEPTH, D_MODEL, D_MODEL), D_MODEL),
        'x_norm_post': gain(ks[19], (DEPTH, D_MODEL)),
        'ffn_norm_pre': gain(ks[20], (DEPTH, D_MODEL)),
        'ffn_w_up': dense(ks[21], (DEPTH, D_MODEL, 2 * D_FF), D_MODEL),
        'ffn_conv_w': dense(ks[22], (DEPTH, FFN_KERNEL, 2 * D_FF), FFN_KERNEL),
        'ffn_conv_b': bias(ks[23], (DEPTH, 2 * D_FF)),
        'ffn_w_down': dense(ks[24], (DEPTH, D_FF, D_MODEL), D_FF),
        'ffn_norm_post': gain(ks[25], (DEPTH, D_MODEL)),
    }


def reference(x, mem, positions, mix_norm_pre, w_in, cv_w, cv_b, cv_ln_g, cv_ln_b, cv_pw_w, cv_pw_b,
              w_out, mix_norm_post, x_norm_pre, mem_norm, x_wq, x_wk, x_wv, x_wo, x_norm_post,
              ffn_norm_pre, ffn_w_up, ffn_conv_w, ffn_conv_b, ffn_w_down, ffn_norm_post):
    h = x
    for l in range(DEPTH):
        y = hybrid_mixer(rms_norm(h, mix_norm_pre[l]), positions, w_in[l], cv_w[l], cv_b[l],
                         cv_ln_g[l], cv_ln_b[l], cv_pw_w[l], cv_pw_b[l], w_out[l])
        h = h + rms_norm(y, mix_norm_post[l])
        y = memory_cross_attention(rms_norm(h, x_norm_pre[l]), rms_norm(mem, mem_norm[l]),
                                   x_wq[l], x_wk[l], x_wv[l], x_wo[l])
        h = h + rms_norm(y, x_norm_post[l])
        y = conv_ffn(rms_norm(h, ffn_norm_pre[l]), ffn_w_up[l], ffn_conv_w[l], ffn_conv_b[l], ffn_w_down[l])
        h = h + rms_norm(y, ffn_norm_post[l])
    return h
```

```python
import functools

import jax
import jax.numpy as jnp
from jax import lax
from jax.experimental import pallas as pl
from jax.experimental.pallas import tpu as pltpu

F32 = jnp.float32
BF16 = jnp.bfloat16

EPS = 1e-6
HEAD_DIM = 64
LANES = 128
SB_HEADS = 4
CV_WIDTH = 256
CV_KERNEL = 31
DL_HEADS = 8
DL_STEPS = 128
ROPE_THETA = 10000.0
X_HEADS = 4
FFN_KERNEL = 3

VMEM_LIMIT = 56 * 1024 * 1024

NT_DIMS = (((1,), (1,)), ((), ()))


def _cparams(*sem):
    return pltpu.CompilerParams(dimension_semantics=sem, vmem_limit_bytes=VMEM_LIMIT)


def _rms(x, g):
    ms = jnp.mean(x * x, axis=-1, keepdims=True)
    return x * lax.rsqrt(ms + EPS) * g


def _norm_matmul_kernel(x_ref, g_ref, w_ref, o_ref, *, n_chunk, out_scale):
    xn = _rms(x_ref[...], g_ref[...]).astype(BF16)
    n = w_ref.shape[1]
    for c in range(0, n, n_chunk):
        y = jnp.dot(xn, w_ref[:, c:c + n_chunk], preferred_element_type=F32)
        if out_scale != 1.0:
            y = y * out_scale
        o_ref[:, c:c + n_chunk] = y.astype(o_ref.dtype)


def norm_matmul(x, g, w, *, out_dtype, tm, n_chunk=256, out_scale=1.0):
    m, k = x.shape
    n = w.shape[1]
    return pl.pallas_call(
        functools.partial(_norm_matmul_kernel, n_chunk=n_chunk, out_scale=out_scale),
        grid=(m // tm,),
        in_specs=[pl.BlockSpec((tm, k), lambda i: (i, 0)),
                  pl.BlockSpec((1, k), lambda i: (0, 0)),
                  pl.BlockSpec((k, n), lambda i: (0, 0))],
        out_specs=pl.BlockSpec((tm, n), lambda i: (i, 0)),
        out_shape=jax.ShapeDtypeStruct((m, n), out_dtype),
        compiler_params=_cparams("parallel"),
        name="norm_matmul",
    )(x, g.reshape(1, k), w)


def _matmul_norm_res_kernel(*refs, n_in):
    xs = refs[:n_in]
    ws = refs[n_in:2 * n_in]
    res_ref, g_ref, o_ref = refs[2 * n_in:]
    y = jnp.dot(xs[0][...], ws[0][...], preferred_element_type=F32)
    for x_ref, w_ref in zip(xs[1:], ws[1:]):
        y = y + jnp.dot(x_ref[...], w_ref[...], preferred_element_type=F32)
    o_ref[...] = res_ref[...] + _rms(y, g_ref[...])


def matmul_norm_residual(xs, ws, res, g, *, tm):
    m, d = res.shape
    n_in = len(xs)
    in_specs = [pl.BlockSpec((tm, x.shape[1]), lambda i: (i, 0)) for x in xs]
    in_specs += [pl.BlockSpec(w.shape, lambda i: (0, 0)) for w in ws]
    in_specs += [pl.BlockSpec((tm, d), lambda i: (i, 0)), pl.BlockSpec((1, d), lambda i: (0, 0))]
    return pl.pallas_call(
        functools.partial(_matmul_norm_res_kernel, n_in=n_in),
        grid=(m // tm,),
        in_specs=in_specs,
        out_specs=pl.BlockSpec((tm, d), lambda i: (i, 0)),
        out_shape=jax.ShapeDtypeStruct((m, d), F32),
        compiler_params=_cparams("parallel"),
        name="matmul_norm_residual",
    )(*xs, *ws, res, g.reshape(1, d))


def _sb_kernel(q_ref, k_ref, v_ref, o_ref, kb_ref, vb_ref, *, seq, tq):
    scale = HEAD_DIM ** -0.5
    kb_ref[...] = k_ref[...].astype(BF16)
    vb_ref[...] = v_ref[...].astype(BF16)

    lane = lax.broadcasted_iota(jnp.int32, (tq, LANES), 1)
    head0 = lane < HEAD_DIM
    row = lax.broadcasted_iota(jnp.int32, (tq, tq), 0)
    col = lax.broadcasted_iota(jnp.int32, (tq, tq), 1)
    before = col < row
    later = jnp.where(row > col, 1.0, 0.0).astype(BF16)
    later2 = jnp.concatenate([later, later], axis=0)

    def tile(qh, kblk, vblk, carry, diag):
        z = lax.dot_general(qh, kblk, NT_DIMS, preferred_element_type=F32)
        softplus = jnp.log(1.0 + jnp.exp(-jnp.abs(z)))
        log_beta = jnp.minimum(z, 0.0) - softplus
        log_keep = log_beta - z
        if diag:
            log_keep = jnp.where(before, log_keep, 0.0)
        hi = log_keep.astype(BF16)
        lo = (log_keep - hi.astype(F32)).astype(BF16)
        between = jnp.dot(jnp.concatenate([hi, lo], axis=1), later2,
                          preferred_element_type=F32) + carry
        a = jnp.exp(log_beta + between)
        if diag:
            a = jnp.where(before, a, 0.0)
        pv = jnp.dot(a.astype(BF16), vblk, preferred_element_type=F32)
        return pv, carry + jnp.sum(log_keep, axis=1, keepdims=True)

    def q_body(qb, _):
        q0 = pl.multiple_of(qb * tq, tq)
        q = q_ref[pl.ds(q0, tq), :] * scale
        qh0 = jnp.where(head0, q, 0.0).astype(BF16)
        qh1 = jnp.where(head0, 0.0, q).astype(BF16)
        kd = kb_ref[pl.ds(q0, tq), :]
        vd = vb_ref[pl.ds(q0, tq), :]
        zero = jnp.zeros((tq, 1), F32)
        acc0, c0 = tile(qh0, kd, vd, zero, True)
        acc1, c1 = tile(qh1, kd, vd, zero, True)

        def k_body(j, st):
            a0, a1, c0, c1 = st
            k0 = pl.multiple_of((qb - 1 - j) * tq, tq)
            kblk = kb_ref[pl.ds(k0, tq), :]
            vblk = vb_ref[pl.ds(k0, tq), :]
            p0, c0 = tile(qh0, kblk, vblk, c0, False)
            p1, c1 = tile(qh1, kblk, vblk, c1, False)
            return a0 + p0, a1 + p1, c0, c1

        acc0, acc1, _, _ = lax.fori_loop(0, qb, k_body, (acc0, acc1, c0, c1))
        o_ref[pl.ds(q0, tq), :] = jnp.where(head0, acc0, acc1).astype(o_ref.dtype)
        return 0

    lax.fori_loop(0, seq // tq, q_body, 0)


def stick_breaking(u, *, tq=128):
    b, s, _ = u.shape
    pairs = SB_HEADS // 2
    blk = lambda off: pl.BlockSpec((None, s, LANES), lambda i, p: (i, 0, off + p))
    return pl.pallas_call(
        functools.partial(_sb_kernel, seq=s, tq=tq),
        grid=(b, pairs),
        in_specs=[blk(0), blk(pairs), blk(2 * pairs)],
        out_specs=pl.BlockSpec((None, s, LANES), lambda i, p: (i, 0, p)),
        out_shape=jax.ShapeDtypeStruct((b, s, SB_HEADS * HEAD_DIM), BF16),
        scratch_shapes=[pltpu.VMEM((s, LANES), BF16), pltpu.VMEM((s, LANES), BF16)],
        compiler_params=_cparams("parallel", "parallel"),
        name="stick_breaking",
    )(u, u, u)


CV_HALO = 32


def _conv_kernel(val_ref, gate_ref, hval_ref, hgate_ref, w_ref, b_ref, lng_ref, lnb_ref,
                 pw_ref, pwb_ref, o_ref, g_scr, *, tm):
    first = pl.program_id(1) == 0
    g_scr[CV_HALO:, :] = val_ref[...] * jax.nn.sigmoid(gate_ref[...])
    halo = hval_ref[...] * jax.nn.sigmoid(hgate_ref[...])
    g_scr[:CV_HALO, :] = jnp.where(first, 0.0, halo)
    w = w_ref[...]
    acc = jnp.zeros((tm, CV_WIDTH), F32) + b_ref[...]
    base = CV_HALO - (CV_KERNEL - 1)
    for k in range(CV_KERNEL):
        acc = acc + g_scr[base + k:base + k + tm, :] * w[k:k + 1, :]
    mu = jnp.mean(acc, axis=-1, keepdims=True)
    xc = acc - mu
    var = jnp.mean(xc * xc, axis=-1, keepdims=True)
    y = xc * lax.rsqrt(var + EPS) * lng_ref[...] + lnb_ref[...]
    y = y * jax.nn.sigmoid(y)
    out = jnp.dot(y.astype(BF16), pw_ref[...], preferred_element_type=F32) + pwb_ref[...]
    o_ref[...] = out.astype(o_ref.dtype)


def conformer_conv(u, cv_w, cv_b, ln_g, ln_b, pw_w, pw_b, *, tm=512):
    b, s, _ = u.shape
    c = CV_WIDTH
    val_blk = 3 * SB_HEADS * HEAD_DIM // c
    hpt = tm // CV_HALO
    cur = lambda off: pl.BlockSpec((None, tm, c), lambda i, t: (i, t, off))
    halo = lambda off: pl.BlockSpec((None, CV_HALO, c),
                                    lambda i, t: (i, jnp.maximum(t * hpt - 1, 0), off))
    vec = pl.BlockSpec((1, c), lambda i, t: (0, 0))
    return pl.pallas_call(
        functools.partial(_conv_kernel, tm=tm),
        grid=(b, s // tm),
        in_specs=[cur(val_blk), cur(val_blk + 1), halo(val_blk), halo(val_blk + 1),
                  pl.BlockSpec((CV_KERNEL, c), lambda i, t: (0, 0)), vec, vec, vec,
                  pl.BlockSpec((c, c), lambda i, t: (0, 0)), vec],
        out_specs=pl.BlockSpec((None, tm, c), lambda i, t: (i, t, 0)),
        out_shape=jax.ShapeDtypeStruct((b, s, c), BF16),
        scratch_shapes=[pltpu.VMEM((tm + CV_HALO, c), F32)],
        compiler_params=_cparams("parallel", "parallel"),
        name="conformer_conv",
    )(u, u, u, u, cv_w, cv_b.reshape(1, c), ln_g.reshape(1, c), ln_b.reshape(1, c),
      pw_w, pw_b.reshape(1, c))


def _rope_table_kernel(pos_ref, invf_ref, cos_ref, sin_ref, *, seq):
    half = HEAD_DIM // 2
    ang = invf_ref[...] * pos_ref[...].astype(F32)
    c = jnp.cos(ang)
    s = jnp.sin(ang)
    cos_t = jnp.concatenate([c, c, c, c], axis=0)
    sin_t = jnp.concatenate([-s, s, -s, s], axis=0)
    del half
    for j in range(seq // LANES):
        cos_ref[j * LANES:(j + 1) * LANES, :] = cos_t[:, j * LANES:(j + 1) * LANES].T
        sin_ref[j * LANES:(j + 1) * LANES, :] = sin_t[:, j * LANES:(j + 1) * LANES].T


def rope_tables(positions):
    b, s = positions.shape
    half = HEAD_DIM // 2
    inv_freq = ROPE_THETA ** (-jnp.arange(half, dtype=F32) / half)
    out = jax.ShapeDtypeStruct((b, s, LANES), F32)
    return pl.pallas_call(
        functools.partial(_rope_table_kernel, seq=s),
        grid=(b,),
        in_specs=[pl.BlockSpec((None, 1, s), lambda i: (i, 0, 0)),
                  pl.BlockSpec((half, 1), lambda i: (0, 0))],
        out_specs=[pl.BlockSpec((None, s, LANES), lambda i: (i, 0, 0))] * 2,
        out_shape=[out, out],
        compiler_params=_cparams("parallel"),
        name="rope_tables",
    )(positions.reshape(b, 1, s), inv_freq.reshape(half, 1))


DL_DILATIONS = (1, 4, 16)
PAD_NAT = DL_STEPS
PAD_D4 = 4 * DL_STEPS


def _dil_kernel(q_ref, k_ref, v_ref, cos_ref, sin_ref, o_ref,
                kn, vn, qd, kd, vd, accd, md, ld, acct, mt, lt, *, seq):
    t = DL_STEPS
    scale = HEAD_DIM ** -0.5
    sub = seq // 4
    lane = lax.broadcasted_iota(jnp.int32, (t, LANES), 1)
    head0 = lane < HEAD_DIM
    qi = lax.broadcasted_iota(jnp.int32, (t, 2 * t), 0)
    kj = lax.broadcasted_iota(jnp.int32, (t, 2 * t), 1)
    band = (kj >= qi) & (kj <= qi + t)

    def rope(x, cos, sin):
        ln = lax.broadcasted_iota(jnp.int32, x.shape, 1)
        first_half = (ln % HEAD_DIM) < (HEAD_DIM // 2)
        partner = jnp.where(first_half, pltpu.roll(x, LANES - HEAD_DIM // 2, axis=1),
                            pltpu.roll(x, HEAD_DIM // 2, axis=1))
        return x * cos + partner * sin

    kn[:PAD_NAT, :] = jnp.zeros((PAD_NAT, LANES), F32)
    vn[:PAD_NAT, :] = jnp.zeros((PAD_NAT, LANES), F32)
    kd[:PAD_D4, :] = jnp.zeros((PAD_D4, LANES), F32)
    vd[:PAD_D4, :] = jnp.zeros((PAD_D4, LANES), F32)
    ch = 256

    def nat_body(c, _):
        r0 = pl.multiple_of(c * ch, ch)
        rows = pl.ds(r0, ch)
        kn[pl.ds(PAD_NAT + r0, ch), :] = rope(k_ref[rows, :], cos_ref[rows, :], sin_ref[rows, :])
        vn[pl.ds(PAD_NAT + r0, ch), :] = v_ref[rows, :]
        return 0

    lax.fori_loop(0, seq // ch, nat_body, 0)

    def d4_body(c, _):
        r4 = c // (sub // ch)
        c0 = (c % (sub // ch)) * ch
        src = pl.ds(r4 + 4 * c0, ch, stride=4)
        dst0 = pl.multiple_of(r4 * sub + c0, ch)
        qd[pl.ds(dst0, ch), :] = rope(q_ref[src, :], cos_ref[src, :], sin_ref[src, :]) * scale
        kd[pl.ds(PAD_D4 + dst0, ch), :] = kn[pl.ds(PAD_NAT + r4 + 4 * c0, ch, stride=4), :]
        vd[pl.ds(PAD_D4 + dst0, ch), :] = v_ref[src, :]
        return 0

    lax.fori_loop(0, seq // ch, d4_body, 0)

    def tile(q, kw, vw, valid, prev):
        kb = kw.astype(BF16)
        vb = vw.astype(BF16)
        q0 = jnp.where(head0, q, 0.0).astype(BF16)
        q1 = jnp.where(head0, 0.0, q).astype(BF16)
        s0 = jnp.where(valid, lax.dot_general(q0, kb, NT_DIMS, preferred_element_type=F32), -jnp.inf)
        s1 = jnp.where(valid, lax.dot_general(q1, kb, NT_DIMS, preferred_element_type=F32), -jnp.inf)
        m0 = jnp.max(s0, axis=1, keepdims=True)
        m1 = jnp.max(s1, axis=1, keepdims=True)
        if prev is not None:
            m_prev, l_prev, acc_prev = prev
            m0 = jnp.maximum(m0, m_prev[:, 0:1])
            m1 = jnp.maximum(m1, m_prev[:, HEAD_DIM:HEAD_DIM + 1])
        p0 = jnp.exp(s0 - m0)
        p1 = jnp.exp(s1 - m1)
        l_new = jnp.where(head0, jnp.sum(p0, axis=1, keepdims=True), jnp.sum(p1, axis=1, keepdims=True))
        acc_new = jnp.where(head0, jnp.dot(p0.astype(BF16), vb, preferred_element_type=F32),
                            jnp.dot(p1.astype(BF16), vb, preferred_element_type=F32))
        m_new = jnp.where(head0, m0, m1)
        if prev is not None:
            alpha = jnp.exp(m_prev - m_new)
            l_new = alpha * l_prev + l_new
            acc_new = alpha * acc_prev + acc_new
        return m_new, l_new, acc_new

    def valid_mask(n):
        return band & (kj >= jnp.where(n == 0, t, 0))

    def d4_tile(i, _):
        r0 = pl.multiple_of(i * t, t)
        n = i % (sub // t)
        m_new, l_new, acc_new = tile(qd[pl.ds(r0, t), :],
                                     kd[pl.ds(PAD_D4 + r0 - t, 2 * t), :],
                                     vd[pl.ds(PAD_D4 + r0 - t, 2 * t), :], valid_mask(n), None)
        md[pl.ds(r0, t), :] = m_new
        ld[pl.ds(r0, t), :] = l_new
        accd[pl.ds(r0, t), :] = acc_new
        return 0

    lax.fori_loop(0, seq // t, d4_tile, 0)

    nb16 = seq // 16 // t

    def d16_tile(i, _):
        r4 = i // (4 * nb16)
        s4 = (i // nb16) % 4
        n = i % nb16
        base = r4 * sub + s4 + 4 * t * n
        rows = pl.ds(base, t, stride=4)
        krows = pl.ds(PAD_D4 + base - 4 * t, 2 * t, stride=4)
        prev = (md[rows, :], ld[rows, :], accd[rows, :])
        m_new, l_new, acc_new = tile(qd[rows, :], kd[krows, :], vd[krows, :], valid_mask(n), prev)
        md[rows, :] = m_new
        ld[rows, :] = l_new
        accd[rows, :] = acc_new
        return 0

    lax.fori_loop(0, seq // t, d16_tile, 0)

    def d1_tile(n, _):
        r0 = pl.multiple_of(n * t, t)
        rows = pl.ds(r0, t)
        for r4 in range(4):
            src = pl.ds(r4 * sub + n * (t // 4), t // 4)
            dst = pl.ds(r4, t // 4, stride=4)
            mt[dst, :] = md[src, :]
            lt[dst, :] = ld[src, :]
            acct[dst, :] = accd[src, :]
        q = rope(q_ref[rows, :], cos_ref[rows, :], sin_ref[rows, :]) * scale
        _, l_new, acc_new = tile(q, kn[pl.ds(PAD_NAT + r0 - t, 2 * t), :],
                                 vn[pl.ds(PAD_NAT + r0 - t, 2 * t), :], valid_mask(n),
                                 (mt[...], lt[...], acct[...]))
        o_ref[rows, :] = (acc_new / l_new).astype(o_ref.dtype)
        return 0

    lax.fori_loop(0, seq // t, d1_tile, 0)


def dilated_attention(u, cos_t, sin_t):
    b, s, _ = u.shape
    assert s % (16 * DL_STEPS) == 0
    pairs = DL_HEADS // 2
    first = (3 * SB_HEADS * HEAD_DIM + 2 * CV_WIDTH) // LANES
    blk = lambda off: pl.BlockSpec((None, s, LANES), lambda i, p: (i, 0, off + p))
    tab = pl.BlockSpec((None, s, LANES), lambda i, p: (i, 0, 0))
    big = lambda pad: pltpu.VMEM((pad + s, LANES), F32)
    small = pltpu.VMEM((DL_STEPS, LANES), F32)
    return pl.pallas_call(
        functools.partial(_dil_kernel, seq=s),
        grid=(b, pairs),
        in_specs=[blk(first), blk(first + pairs), blk(first + 2 * pairs), tab, tab],
        out_specs=pl.BlockSpec((None, s, LANES), lambda i, p: (i, 0, p)),
        out_shape=jax.ShapeDtypeStruct((b, s, DL_HEADS * HEAD_DIM), BF16),
        scratch_shapes=[big(PAD_NAT), big(PAD_NAT), big(0), big(PAD_D4), big(PAD_D4),
                        big(0), big(0), big(0), small, small, small],
        compiler_params=_cparams("parallel", "parallel"),
        name="dilated_attention",
    )(u, u, u, cos_t, sin_t)


def _xattn_kernel(q_ref, kv_ref, o_ref, *, d_model):
    hd = d_model // X_HEADS
    for h in range(X_HEADS):
        q = q_ref[:, h * hd:(h + 1) * hd]
        k = kv_ref[:, h * hd:(h + 1) * hd]
        v = kv_ref[:, d_model + h * hd:d_model + (h + 1) * hd]
        s = lax.dot_general(q, k, NT_DIMS, preferred_element_type=F32)
        p = jnp.exp(s - jnp.max(s, axis=1, keepdims=True))
        den = jnp.sum(p, axis=1, keepdims=True)
        o = jnp.dot(p.astype(BF16), v, preferred_element_type=F32) / den
        o_ref[:, h * hd:(h + 1) * hd] = o.astype(o_ref.dtype)


def cross_attention(q, kv, *, tm=512):
    b, s, d = q.shape
    n_mem = kv.shape[1]
    return pl.pallas_call(
        functools.partial(_xattn_kernel, d_model=d),
        grid=(b, s // tm),
        in_specs=[pl.BlockSpec((None, tm, d), lambda i, t: (i, t, 0)),
                  pl.BlockSpec((None, n_mem, 2 * d), lambda i, t: (i, 0, 0))],
        out_specs=pl.BlockSpec((None, tm, d), lambda i, t: (i, t, 0)),
        out_shape=jax.ShapeDtypeStruct((b, s, d), BF16),
        compiler_params=_cparams("parallel", "parallel"),
        name="cross_attention",
    )(q, kv)


FFN_HALO = 8


def _ffn_kernel(h_ref, halo_ref, gpre_ref, wup_ref, cw_ref, cb_ref, wdn_ref, gpost_ref, o_ref,
                *, tm, d_ff, n_chunk):
    first = pl.program_id(1) == 0
    h = h_ref[...]
    g = gpre_ref[...]
    halo = jnp.where(first, 0.0, _rms(halo_ref[...], g))
    xn = jnp.concatenate([halo, _rms(h, g)], axis=0).astype(BF16)

    def conv(u, c0):
        base = FFN_HALO - (FFN_KERNEL - 1)
        y = cb_ref[:, c0:c0 + n_chunk]
        for k in range(FFN_KERNEL):
            y = y + u[base + k:base + k + tm, :] * cw_ref[k:k + 1, c0:c0 + n_chunk]
        return y

    y = jnp.zeros((tm, h.shape[1]), F32)
    for c0 in range(0, d_ff, n_chunk):
        ug = jnp.dot(xn, wup_ref[:, c0:c0 + n_chunk], preferred_element_type=F32)
        uv = jnp.dot(xn, wup_ref[:, d_ff + c0:d_ff + c0 + n_chunk], preferred_element_type=F32)
        act = jax.nn.gelu(conv(ug, c0), approximate=True) * conv(uv, d_ff + c0)
        y = y + jnp.dot(act.astype(BF16), wdn_ref[c0:c0 + n_chunk, :], preferred_element_type=F32)
    o_ref[...] = h + _rms(y, gpost_ref[...])


def conv_ffn(h, g_pre, w_up, conv_w, conv_b, w_down, g_post, *, tm=512, n_chunk=256):
    b, s, d = h.shape
    d_ff = w_down.shape[0]
    hpt = tm // FFN_HALO
    const = lambda shape: pl.BlockSpec(shape, lambda i, t: (0, 0))
    return pl.pallas_call(
        functools.partial(_ffn_kernel, tm=tm, d_ff=d_ff, n_chunk=n_chunk),
        grid=(b, s // tm),
        in_specs=[pl.BlockSpec((None, tm, d), lambda i, t: (i, t, 0)),
                  pl.BlockSpec((None, FFN_HALO, d), lambda i, t: (i, jnp.maximum(t * hpt - 1, 0), 0)),
                  const((1, d)), const((d, 2 * d_ff)), const((FFN_KERNEL, 2 * d_ff)),
                  const((1, 2 * d_ff)), const((d_ff, d)), const((1, d))],
        out_specs=pl.BlockSpec((None, tm, d), lambda i, t: (i, t, 0)),
        out_shape=jax.ShapeDtypeStruct((b, s, d), F32),
        compiler_params=_cparams("parallel", "parallel"),
        name="conv_ffn",
    )(h, h, g_pre.reshape(1, d), w_up, conv_w, conv_b.reshape(1, 2 * d_ff), w_down,
      g_post.reshape(1, d))


def kernel(x, mem, positions, mix_norm_pre, w_in, cv_w, cv_b, cv_ln_g, cv_ln_b, cv_pw_w, cv_pw_b,
           w_out, mix_norm_post, x_norm_pre, mem_norm, x_wq, x_wk, x_wv, x_wo, x_norm_post,
           ffn_norm_pre, ffn_w_up, ffn_conv_w, ffn_conv_b, ffn_w_down, ffn_norm_post):
    b, s, d = x.shape
    n_mem = mem.shape[1]
    depth = w_in.shape[0]
    m = b * s
    sb_w = SB_HEADS * HEAD_DIM
    tm = 512

    cos_t, sin_t = rope_tables(positions)
    mem2 = mem.reshape(b * n_mem, d)
    h = x
    for l in range(depth):
        u = norm_matmul(h.reshape(m, d), mix_norm_pre[l], w_in[l].astype(BF16),
                        out_dtype=F32, tm=tm).reshape(b, s, -1)
        a_out = stick_breaking(u)
        b_out = conformer_conv(u, cv_w[l], cv_b[l], cv_ln_g[l], cv_ln_b[l],
                               cv_pw_w[l].astype(BF16), cv_pw_b[l])
        c_out = dilated_attention(u, cos_t, sin_t)
        wo = w_out[l].astype(BF16)
        h = matmul_norm_residual(
            [a_out.reshape(m, -1), b_out.reshape(m, -1), c_out.reshape(m, -1)],
            [wo[:sb_w], wo[sb_w:sb_w + CV_WIDTH], wo[sb_w + CV_WIDTH:]],
            h.reshape(m, d), mix_norm_post[l], tm=tm)

        q = norm_matmul(h, x_norm_pre[l], x_wq[l].astype(BF16), out_dtype=BF16, tm=tm,
                        out_scale=(d // X_HEADS) ** -0.5).reshape(b, s, d)
        wkv = jnp.concatenate([x_wk[l], x_wv[l]], axis=1).astype(BF16)
        kv = norm_matmul(mem2, mem_norm[l], wkv, out_dtype=BF16, tm=tm).reshape(b, n_mem, 2 * d)
        o = cross_attention(q, kv)
        h = matmul_norm_residual([o.reshape(m, d)], [x_wo[l].astype(BF16)], h, x_norm_post[l], tm=tm)

        h = conv_ffn(h.reshape(b, s, d), ffn_norm_pre[l], ffn_w_up[l].astype(BF16), ffn_conv_w[l],
                     ffn_conv_b[l], ffn_w_down[l].astype(BF16), ffn_norm_post[l])
    return h
```

```python
import functools

import jax
import jax.numpy as jnp
from jax import lax
from jax.experimental import pallas as pl
from jax.experimental.pallas import tpu as pltpu

F32 = jnp.float32
BF16 = jnp.bfloat16

EPS = 1e-6
HEAD_DIM = 64
LANES = 128
SB_HEADS = 4
CV_WIDTH = 256
CV_KERNEL = 31
DL_HEADS = 8
DL_STEPS = 128
ROPE_THETA = 10000.0
X_HEADS = 4
FFN_KERNEL = 3

VMEM_LIMIT = 56 * 1024 * 1024

NT_DIMS = (((1,), (1,)), ((), ()))


def _cparams(*sem):
    return pltpu.CompilerParams(dimension_semantics=sem, vmem_limit_bytes=VMEM_LIMIT)


def _rms(x, g):
    ms = jnp.mean(x * x, axis=-1, keepdims=True)
    return x * lax.rsqrt(ms + EPS) * g


def _norm_matmul_kernel(x_ref, g_ref, w_ref, o_ref, *, n_chunk, out_scale):
    xn = _rms(x_ref[...], g_ref[...]).astype(BF16)
    n = w_ref.shape[1]
    for c in range(0, n, n_chunk):
        y = jnp.dot(xn, w_ref[:, c:c + n_chunk], preferred_element_type=F32)
        if out_scale != 1.0:
            y = y * out_scale
        o_ref[:, c:c + n_chunk] = y.astype(o_ref.dtype)


def norm_matmul(x, g, w, *, out_dtype, tm, n_chunk=256, out_scale=1.0):
    m, k = x.shape
    n = w.shape[1]
    return pl.pallas_call(
        functools.partial(_norm_matmul_kernel, n_chunk=n_chunk, out_scale=out_scale),
        grid=(m // tm,),
        in_specs=[pl.BlockSpec((tm, k), lambda i: (i, 0)),
                  pl.BlockSpec((1, k), lambda i: (0, 0)),
                  pl.BlockSpec((k, n), lambda i: (0, 0))],
        out_specs=pl.BlockSpec((tm, n), lambda i: (i, 0)),
        out_shape=jax.ShapeDtypeStruct((m, n), out_dtype),
        compiler_params=_cparams("parallel"),
        name="norm_matmul",
    )(x, g.reshape(1, k), w)


def _matmul_norm_res_kernel(*refs, n_in):
    xs = refs[:n_in]
    ws = refs[n_in:2 * n_in]
    res_ref, g_ref, o_ref = refs[2 * n_in:]
    y = jnp.dot(xs[0][...], ws[0][...], preferred_element_type=F32)
    for x_ref, w_ref in zip(xs[1:], ws[1:]):
        y = y + jnp.dot(x_ref[...], w_ref[...], preferred_element_type=F32)
    o_ref[...] = res_ref[...] + _rms(y, g_ref[...])


def matmul_norm_residual(xs, ws, res, g, *, tm):
    m, d = res.shape
    n_in = len(xs)
    in_specs = [pl.BlockSpec((tm, x.shape[1]), lambda i: (i, 0)) for x in xs]
    in_specs += [pl.BlockSpec(w.shape, lambda i: (0, 0)) for w in ws]
    in_specs += [pl.BlockSpec((tm, d), lambda i: (i, 0)), pl.BlockSpec((1, d), lambda i: (0, 0))]
    return pl.pallas_call(
        functools.partial(_matmul_norm_res_kernel, n_in=n_in),
        grid=(m // tm,),
        in_specs=in_specs,
        out_specs=pl.BlockSpec((tm, d), lambda i: (i, 0)),
        out_shape=jax.ShapeDtypeStruct((m, d), F32),
        compiler_params=_cparams("parallel"),
        name="matmul_norm_residual",
    )(*xs, *ws, res, g.reshape(1, d))


SB_MASKED_Z = -1e4
SB_DEAD = -104.0


def _sb_kernel(q_ref, k_ref, v_ref, o_ref, kt_ref, vb_ref, *, seq, tq, n_fast):
    scale = HEAD_DIM ** -0.5
    vb_ref[...] = v_ref[...].astype(BF16)

    def transpose_keys(c, _):
        c0 = pl.multiple_of(c * LANES, LANES)
        kt_ref[:, pl.ds(c0, LANES)] = k_ref[pl.ds(c0, LANES), :].T.astype(BF16)
        return 0

    lax.fori_loop(0, seq // LANES, transpose_keys, 0)

    head0 = lax.broadcasted_iota(jnp.int32, (tq, LANES), 1) < HEAD_DIM
    row = lax.broadcasted_iota(jnp.int32, (2 * tq, tq), 0) % tq
    col = lax.broadcasted_iota(jnp.int32, (2 * tq, tq), 1)
    before = col < row
    krow = lax.broadcasted_iota(jnp.int32, (tq, tq), 0)
    kcol = lax.broadcasted_iota(jnp.int32, (tq, tq), 1)
    later = jnp.where(krow > kcol, 1.0, 0.0).astype(BF16)
    later2 = jnp.concatenate([later, later], axis=0)

    def tile(q2, kt_blk, vblk, carry, ok):
        z = jnp.dot(q2, kt_blk, preferred_element_type=F32)
        if ok is not None:
            z = jnp.where(ok, z, SB_MASKED_Z)
        softplus = jnp.log(1.0 + jnp.exp(-jnp.abs(z)))
        log_beta = jnp.minimum(z, 0.0) - softplus
        log_keep = log_beta - z
        hi = log_keep.astype(BF16)
        lo = (log_keep - hi.astype(F32)).astype(BF16)
        between = jnp.dot(jnp.concatenate([hi, lo], axis=1), later2,
                          preferred_element_type=F32) + carry
        a = jnp.exp(log_beta + between)
        pv = jnp.dot(a.astype(BF16), vblk, preferred_element_type=F32)
        return pv, carry + jnp.sum(log_keep, axis=1, keepdims=True)

    def q_body(qb, _):
        q0 = pl.multiple_of(qb * tq, tq)
        q = q_ref[pl.ds(q0, tq), :] * scale
        q2 = jnp.concatenate([jnp.where(head0, q, 0.0), jnp.where(head0, 0.0, q)], axis=0).astype(BF16)
        acc = jnp.zeros((2 * tq, LANES), F32)
        carry = jnp.zeros((2 * tq, 1), F32)
        for j in range(n_fast):
            kblock = qb - j
            k0 = pl.multiple_of(jnp.maximum(kblock, 0) * tq, tq)
            ok = before if j == 0 else kblock >= 0
            pv, carry = tile(q2, kt_ref[:, pl.ds(k0, tq)], vb_ref[pl.ds(k0, tq), :], carry, ok)
            acc = acc + pv

        def alive(st):
            kblock, _, c = st
            return (kblock >= 0) & (jnp.max(c) > SB_DEAD)

        def k_body(st):
            kblock, a, c = st
            k0 = pl.multiple_of(kblock * tq, tq)
            pv, c = tile(q2, kt_ref[:, pl.ds(k0, tq)], vb_ref[pl.ds(k0, tq), :], c, None)
            return kblock - 1, a + pv, c

        _, acc, _ = lax.while_loop(alive, k_body, (qb - n_fast, acc, carry))
        o_ref[pl.ds(q0, tq), :] = jnp.where(head0, acc[:tq], acc[tq:]).astype(o_ref.dtype)
        return 0

    lax.fori_loop(0, seq // tq, q_body, 0)


def stick_breaking(u, *, tq=128, n_fast=3):
    b, s, _ = u.shape
    pairs = SB_HEADS // 2
    blk = lambda off: pl.BlockSpec((None, s, LANES), lambda i, p: (i, 0, off + p))
    return pl.pallas_call(
        functools.partial(_sb_kernel, seq=s, tq=tq, n_fast=n_fast),
        grid=(b, pairs),
        in_specs=[blk(0), blk(pairs), blk(2 * pairs)],
        out_specs=pl.BlockSpec((None, s, LANES), lambda i, p: (i, 0, p)),
        out_shape=jax.ShapeDtypeStruct((b, s, SB_HEADS * HEAD_DIM), BF16),
        scratch_shapes=[pltpu.VMEM((LANES, s), BF16), pltpu.VMEM((s, LANES), BF16)],
        compiler_params=_cparams("parallel", "parallel"),
        name="stick_breaking",
    )(u, u, u)


CV_HALO = 32


def _conv_kernel(val_ref, gate_ref, hval_ref, hgate_ref, w_ref, b_ref, lng_ref, lnb_ref,
                 pw_ref, pwb_ref, o_ref, g_scr, *, tm):
    first = pl.program_id(1) == 0
    g_scr[CV_HALO:, :] = val_ref[...] * jax.nn.sigmoid(gate_ref[...])
    halo = hval_ref[...] * jax.nn.sigmoid(hgate_ref[...])
    g_scr[:CV_HALO, :] = jnp.where(first, 0.0, halo)
    w = w_ref[...]
    acc = jnp.zeros((tm, CV_WIDTH), F32) + b_ref[...]
    base = CV_HALO - (CV_KERNEL - 1)
    for k in range(CV_KERNEL):
        acc = acc + g_scr[base + k:base + k + tm, :] * w[k:k + 1, :]
    mu = jnp.mean(acc, axis=-1, keepdims=True)
    xc = acc - mu
    var = jnp.mean(xc * xc, axis=-1, keepdims=True)
    y = xc * lax.rsqrt(var + EPS) * lng_ref[...] + lnb_ref[...]
    y = y * jax.nn.sigmoid(y)
    out = jnp.dot(y.astype(BF16), pw_ref[...], preferred_element_type=F32) + pwb_ref[...]
    o_ref[...] = out.astype(o_ref.dtype)


def conformer_conv(u, cv_w, cv_b, ln_g, ln_b, pw_w, pw_b, *, tm=512):
    b, s, _ = u.shape
    c = CV_WIDTH
    val_blk = 3 * SB_HEADS * HEAD_DIM // c
    hpt = tm // CV_HALO
    cur = lambda off: pl.BlockSpec((None, tm, c), lambda i, t: (i, t, off))
    halo = lambda off: pl.BlockSpec((None, CV_HALO, c),
                                    lambda i, t: (i, jnp.maximum(t * hpt - 1, 0), off))
    vec = pl.BlockSpec((1, c), lambda i, t: (0, 0))
    return pl.pallas_call(
        functools.partial(_conv_kernel, tm=tm),
        grid=(b, s // tm),
        in_specs=[cur(val_blk), cur(val_blk + 1), halo(val_blk), halo(val_blk + 1),
                  pl.BlockSpec((CV_KERNEL, c), lambda i, t: (0, 0)), vec, vec, vec,
                  pl.BlockSpec((c, c), lambda i, t: (0, 0)), vec],
        out_specs=pl.BlockSpec((None, tm, c), lambda i, t: (i, t, 0)),
        out_shape=jax.ShapeDtypeStruct((b, s, c), BF16),
        scratch_shapes=[pltpu.VMEM((tm + CV_HALO, c), F32)],
        compiler_params=_cparams("parallel", "parallel"),
        name="conformer_conv",
    )(u, u, u, u, cv_w, cv_b.reshape(1, c), ln_g.reshape(1, c), ln_b.reshape(1, c),
      pw_w, pw_b.reshape(1, c))


def _rope_table_kernel(pos_ref, invf_ref, cos_ref, sin_ref, *, seq):
    half = HEAD_DIM // 2
    ang = invf_ref[...] * pos_ref[...].astype(F32)
    c = jnp.cos(ang)
    s = jnp.sin(ang)
    cos_t = jnp.concatenate([c, c, c, c], axis=0)
    sin_t = jnp.concatenate([-s, s, -s, s], axis=0)
    del half
    for j in range(seq // LANES):
        cos_ref[j * LANES:(j + 1) * LANES, :] = cos_t[:, j * LANES:(j + 1) * LANES].T
        sin_ref[j * LANES:(j + 1) * LANES, :] = sin_t[:, j * LANES:(j + 1) * LANES].T


def rope_tables(positions):
    b, s = positions.shape
    half = HEAD_DIM // 2
    inv_freq = ROPE_THETA ** (-jnp.arange(half, dtype=F32) / half)
    out = jax.ShapeDtypeStruct((b, s, LANES), F32)
    return pl.pallas_call(
        functools.partial(_rope_table_kernel, seq=s),
        grid=(b,),
        in_specs=[pl.BlockSpec((None, 1, s), lambda i: (i, 0, 0)),
                  pl.BlockSpec((half, 1), lambda i: (0, 0))],
        out_specs=[pl.BlockSpec((None, s, LANES), lambda i: (i, 0, 0))] * 2,
        out_shape=[out, out],
        compiler_params=_cparams("parallel"),
        name="rope_tables",
    )(positions.reshape(b, 1, s), inv_freq.reshape(half, 1))


DL_GROUP = 4
PAD_NAT = DL_STEPS
PAD_D4 = 4 * DL_STEPS


def _dil_kernel(q_ref, k_ref, v_ref, cos_ref, sin_ref, o_ref,
                kn, vn, qd, kd, vd, accd, md, ld, acct, mt, lt, *, seq):
    t = DL_STEPS
    scale = HEAD_DIM ** -0.5
    sub = seq // 4
    lane = lax.broadcasted_iota(jnp.int32, (t, LANES), 1)
    head0 = lane < HEAD_DIM
    qi = lax.broadcasted_iota(jnp.int32, (t, 2 * t), 0)
    kj = lax.broadcasted_iota(jnp.int32, (t, 2 * t), 1)
    band = (kj >= qi) & (kj <= qi + t)

    def rope(x, cos, sin):
        ln = lax.broadcasted_iota(jnp.int32, x.shape, 1)
        first_half = (ln % HEAD_DIM) < (HEAD_DIM // 2)
        partner = jnp.where(first_half, pltpu.roll(x, LANES - HEAD_DIM // 2, axis=1),
                            pltpu.roll(x, HEAD_DIM // 2, axis=1))
        return x * cos + partner * sin

    kn[:PAD_NAT, :] = jnp.zeros((PAD_NAT, LANES), F32)
    vn[:PAD_NAT, :] = jnp.zeros((PAD_NAT, LANES), F32)
    kd[:PAD_D4, :] = jnp.zeros((PAD_D4, LANES), F32)
    vd[:PAD_D4, :] = jnp.zeros((PAD_D4, LANES), F32)
    ch = 256

    def nat_body(c, _):
        r0 = pl.multiple_of(c * ch, ch)
        rows = pl.ds(r0, ch)
        kn[pl.ds(PAD_NAT + r0, ch), :] = rope(k_ref[rows, :], cos_ref[rows, :], sin_ref[rows, :])
        vn[pl.ds(PAD_NAT + r0, ch), :] = v_ref[rows, :]
        return 0

    lax.fori_loop(0, seq // ch, nat_body, 0)

    def d4_body(c, _):
        r4 = c // (sub // ch)
        c0 = (c % (sub // ch)) * ch
        src = pl.ds(r4 + 4 * c0, ch, stride=4)
        dst0 = pl.multiple_of(r4 * sub + c0, ch)
        qd[pl.ds(dst0, ch), :] = rope(q_ref[src, :], cos_ref[src, :], sin_ref[src, :]) * scale
        kd[pl.ds(PAD_D4 + dst0, ch), :] = kn[pl.ds(PAD_NAT + r4 + 4 * c0, ch, stride=4), :]
        vd[pl.ds(PAD_D4 + dst0, ch), :] = v_ref[src, :]
        return 0

    lax.fori_loop(0, seq // ch, d4_body, 0)

    def tile(q, kw, vw, valid, prev):
        kb = kw.astype(BF16)
        vb = vw.astype(BF16)
        q0 = jnp.where(head0, q, 0.0).astype(BF16)
        q1 = jnp.where(head0, 0.0, q).astype(BF16)
        s0 = jnp.where(valid, lax.dot_general(q0, kb, NT_DIMS, preferred_element_type=F32), -jnp.inf)
        s1 = jnp.where(valid, lax.dot_general(q1, kb, NT_DIMS, preferred_element_type=F32), -jnp.inf)
        m0 = jnp.max(s0, axis=1, keepdims=True)
        m1 = jnp.max(s1, axis=1, keepdims=True)
        if prev is not None:
            m_prev, l_prev, acc_prev = prev
            m0 = jnp.maximum(m0, m_prev[:, 0:1])
            m1 = jnp.maximum(m1, m_prev[:, HEAD_DIM:HEAD_DIM + 1])
        p0 = jnp.exp(s0 - m0)
        p1 = jnp.exp(s1 - m1)
        l_new = jnp.where(head0, jnp.sum(p0, axis=1, keepdims=True), jnp.sum(p1, axis=1, keepdims=True))
        acc_new = jnp.where(head0, jnp.dot(p0.astype(BF16), vb, preferred_element_type=F32),
                            jnp.dot(p1.astype(BF16), vb, preferred_element_type=F32))
        m_new = jnp.where(head0, m0, m1)
        if prev is not None:
            alpha = jnp.exp(m_prev - m_new)
            l_new = alpha * l_prev + l_new
            acc_new = alpha * acc_prev + acc_new
        return m_new, l_new, acc_new

    band_first = band & (kj >= t)

    def valid_mask(n):
        return band & (kj >= jnp.where(n == 0, t, 0))

    grp = DL_GROUP
    n_trips = seq // t // grp

    def d4_group(i, _):
        n0 = (i * grp) % (sub // t)
        outs = []
        for g in range(grp):
            r0 = pl.multiple_of((i * grp + g) * t, t)
            out = tile(qd[pl.ds(r0, t), :], kd[pl.ds(PAD_D4 + r0 - t, 2 * t), :],
                       vd[pl.ds(PAD_D4 + r0 - t, 2 * t), :], valid_mask(n0) if g == 0 else band, None)
            outs.append((r0, out))
        for r0, (m_new, l_new, acc_new) in outs:
            md[pl.ds(r0, t), :] = m_new
            ld[pl.ds(r0, t), :] = l_new
            accd[pl.ds(r0, t), :] = acc_new
        return 0

    lax.fori_loop(0, n_trips, d4_group, 0)

    nb16 = seq // 16 // t

    def d16_group(i, _):
        outs = []
        for g in range(grp):
            idx = i * grp + g
            r4 = idx // (4 * nb16)
            s4 = (idx // nb16) % 4
            n = g % nb16
            base = r4 * sub + s4 + 4 * t * n
            rows = pl.ds(base, t, stride=4)
            krows = pl.ds(PAD_D4 + base - 4 * t, 2 * t, stride=4)
            prev = (md[rows, :], ld[rows, :], accd[rows, :])
            out = tile(qd[rows, :], kd[krows, :], vd[krows, :], band_first if n == 0 else band, prev)
            outs.append((rows, out))
        for rows, (m_new, l_new, acc_new) in outs:
            md[rows, :] = m_new
            ld[rows, :] = l_new
            accd[rows, :] = acc_new
        return 0

    lax.fori_loop(0, n_trips, d16_group, 0)

    def d1_group(i, _):
        outs = []
        for g in range(grp):
            n = i * grp + g
            r0 = pl.multiple_of(n * t, t)
            rows = pl.ds(r0, t)
            for r4 in range(4):
                src = pl.ds(r4 * sub + n * (t // 4), t // 4)
                dst = pl.ds(g * t + r4, t // 4, stride=4)
                mt[dst, :] = md[src, :]
                lt[dst, :] = ld[src, :]
                acct[dst, :] = accd[src, :]
            tmp = pl.ds(g * t, t)
            q = rope(q_ref[rows, :], cos_ref[rows, :], sin_ref[rows, :]) * scale
            _, l_new, acc_new = tile(q, kn[pl.ds(PAD_NAT + r0 - t, 2 * t), :],
                                     vn[pl.ds(PAD_NAT + r0 - t, 2 * t), :],
                                     valid_mask(n) if g == 0 else band,
                                     (mt[tmp, :], lt[tmp, :], acct[tmp, :]))
            outs.append((rows, acc_new / l_new))
        for rows, out in outs:
            o_ref[rows, :] = out.astype(o_ref.dtype)
        return 0

    lax.fori_loop(0, n_trips, d1_group, 0)


def dilated_attention(u, cos_t, sin_t):
    b, s, _ = u.shape
    assert s % (16 * DL_STEPS) == 0 and (s // 4 // DL_STEPS) % DL_GROUP == 0
    assert DL_GROUP % (s // 16 // DL_STEPS) == 0
    pairs = DL_HEADS // 2
    first = (3 * SB_HEADS * HEAD_DIM + 2 * CV_WIDTH) // LANES
    blk = lambda off: pl.BlockSpec((None, s, LANES), lambda i, p: (i, 0, off + p))
    tab = pl.BlockSpec((None, s, LANES), lambda i, p: (i, 0, 0))
    big = lambda pad: pltpu.VMEM((pad + s, LANES), F32)
    small = pltpu.VMEM((DL_GROUP * DL_STEPS, LANES), F32)
    return pl.pallas_call(
        functools.partial(_dil_kernel, seq=s),
        grid=(b, pairs),
        in_specs=[blk(first), blk(first + pairs), blk(first + 2 * pairs), tab, tab],
        out_specs=pl.BlockSpec((None, s, LANES), lambda i, p: (i, 0, p)),
        out_shape=jax.ShapeDtypeStruct((b, s, DL_HEADS * HEAD_DIM), BF16),
        scratch_shapes=[big(PAD_NAT), big(PAD_NAT), big(0), big(PAD_D4), big(PAD_D4),
                        big(0), big(0), big(0), small, small, small],
        compiler_params=_cparams("parallel", "parallel"),
        name="dilated_attention",
    )(u, u, u, cos_t, sin_t)


def _xattn_kernel(q_ref, kv_ref, o_ref, *, d_model):
    hd = d_model // X_HEADS
    for h in range(X_HEADS):
        q = q_ref[:, h * hd:(h + 1) * hd]
        k = kv_ref[:, h * hd:(h + 1) * hd]
        v = kv_ref[:, d_model + h * hd:d_model + (h + 1) * hd]
        s = lax.dot_general(q, k, NT_DIMS, preferred_element_type=F32)
        p = jnp.exp(s - jnp.max(s, axis=1, keepdims=True))
        den = jnp.sum(p, axis=1, keepdims=True)
        o = jnp.dot(p.astype(BF16), v, preferred_element_type=F32) / den
        o_ref[:, h * hd:(h + 1) * hd] = o.astype(o_ref.dtype)


def cross_attention(q, kv, *, tm=512):
    b, s, d = q.shape
    n_mem = kv.shape[1]
    return pl.pallas_call(
        functools.partial(_xattn_kernel, d_model=d),
        grid=(b, s // tm),
        in_specs=[pl.BlockSpec((None, tm, d), lambda i, t: (i, t, 0)),
                  pl.BlockSpec((None, n_mem, 2 * d), lambda i, t: (i, 0, 0))],
        out_specs=pl.BlockSpec((None, tm, d), lambda i, t: (i, t, 0)),
        out_shape=jax.ShapeDtypeStruct((b, s, d), BF16),
        compiler_params=_cparams("parallel", "parallel"),
        name="cross_attention",
    )(q, kv)


FFN_HALO = 8


def _ffn_kernel(h_ref, halo_ref, gpre_ref, wup_ref, cw_ref, cb_ref, wdn_ref, gpost_ref, o_ref,
                *, tm, d_ff, n_chunk):
    first = pl.program_id(1) == 0
    h = h_ref[...]
    g = gpre_ref[...]
    halo = jnp.where(first, 0.0, _rms(halo_ref[...], g))
    xn = jnp.concatenate([halo, _rms(h, g)], axis=0).astype(BF16)

    def conv(u, c0):
        base = FFN_HALO - (FFN_KERNEL - 1)
        y = cb_ref[:, c0:c0 + n_chunk]
        for k in range(FFN_KERNEL):
            y = y + u[base + k:base + k + tm, :] * cw_ref[k:k + 1, c0:c0 + n_chunk]
        return y

    y = jnp.zeros((tm, h.shape[1]), F32)
    for c0 in range(0, d_ff, n_chunk):
        ug = jnp.dot(xn, wup_ref[:, c0:c0 + n_chunk], preferred_element_type=F32)
        uv = jnp.dot(xn, wup_ref[:, d_ff + c0:d_ff + c0 + n_chunk], preferred_element_type=F32)
        act = jax.nn.gelu(conv(ug, c0), approximate=True) * conv(uv, d_ff + c0)
        y = y + jnp.dot(act.astype(BF16), wdn_ref[c0:c0 + n_chunk, :], preferred_element_type=F32)
    o_ref[...] = h + _rms(y, gpost_ref[...])


def conv_ffn(h, g_pre, w_up, conv_w, conv_b, w_down, g_post, *, tm=512, n_chunk=256):
    b, s, d = h.shape
    d_ff = w_down.shape[0]
    hpt = tm // FFN_HALO
    const = lambda shape: pl.BlockSpec(shape, lambda i, t: (0, 0))
    return pl.pallas_call(
        functools.partial(_ffn_kernel, tm=tm, d_ff=d_ff, n_chunk=n_chunk),
        grid=(b, s // tm),
        in_specs=[pl.BlockSpec((None, tm, d), lambda i, t: (i, t, 0)),
                  pl.BlockSpec((None, FFN_HALO, d), lambda i, t: (i, jnp.maximum(t * hpt - 1, 0), 0)),
                  const((1, d)), const((d, 2 * d_ff)), const((FFN_KERNEL, 2 * d_ff)),
                  const((1, 2 * d_ff)), const((d_ff, d)), const((1, d))],
        out_specs=pl.BlockSpec((None, tm, d), lambda i, t: (i, t, 0)),
        out_shape=jax.ShapeDtypeStruct((b, s, d), F32),
        compiler_params=_cparams("parallel", "parallel"),
        name="conv_ffn",
    )(h, h, g_pre.reshape(1, d), w_up, conv_w, conv_b.reshape(1, 2 * d_ff), w_down,
      g_post.reshape(1, d))


def kernel(x, mem, positions, mix_norm_pre, w_in, cv_w, cv_b, cv_ln_g, cv_ln_b, cv_pw_w, cv_pw_b,
           w_out, mix_norm_post, x_norm_pre, mem_norm, x_wq, x_wk, x_wv, x_wo, x_norm_post,
           ffn_norm_pre, ffn_w_up, ffn_conv_w, ffn_conv_b, ffn_w_down, ffn_norm_post):
    b, s, d = x.shape
    n_mem = mem.shape[1]
    depth = w_in.shape[0]
    m = b * s
    sb_w = SB_HEADS * HEAD_DIM
    tm = 512

    cos_t, sin_t = rope_tables(positions)
    mem2 = mem.reshape(b * n_mem, d)
    h = x
    for l in range(depth):
        u = norm_matmul(h.reshape(m, d), mix_norm_pre[l], w_in[l].astype(BF16),
                        out_dtype=F32, tm=tm).reshape(b, s, -1)
        a_out = stick_breaking(u)
        b_out = conformer_conv(u, cv_w[l], cv_b[l], cv_ln_g[l], cv_ln_b[l],
                               cv_pw_w[l].astype(BF16), cv_pw_b[l])
        c_out = dilated_attention(u, cos_t, sin_t)
        wo = w_out[l].astype(BF16)
        h = matmul_norm_residual(
            [a_out.reshape(m, -1), b_out.reshape(m, -1), c_out.reshape(m, -1)],
            [wo[:sb_w], wo[sb_w:sb_w + CV_WIDTH], wo[sb_w + CV_WIDTH:]],
            h.reshape(m, d), mix_norm_post[l], tm=tm)

        q = norm_matmul(h, x_norm_pre[l], x_wq[l].astype(BF16), out_dtype=BF16, tm=tm,
                        out_scale=(d // X_HEADS) ** -0.5).reshape(b, s, d)
        wkv = jnp.concatenate([x_wk[l], x_wv[l]], axis=1).astype(BF16)
        kv = norm_matmul(mem2, mem_norm[l], wkv, out_dtype=BF16, tm=tm).reshape(b, n_mem, 2 * d)
        o = cross_attention(q, kv)
        h = matmul_norm_residual([o.reshape(m, d)], [x_wo[l].astype(BF16)], h, x_norm_post[l], tm=tm)

        h = conv_ffn(h.reshape(b, s, d), ffn_norm_pre[l], ffn_w_up[l].astype(BF16), ffn_conv_w[l],
                     ffn_conv_b[l], ffn_w_down[l].astype(BF16), ffn_norm_post[l])
    return h
```

```python
import functools

import jax
import jax.numpy as jnp
from jax import lax
from jax.experimental import pallas as pl
from jax.experimental.pallas import tpu as pltpu

F32 = jnp.float32
BF16 = jnp.bfloat16

EPS = 1e-6
HEAD_DIM = 64
LANES = 128
SB_HEADS = 4
CV_WIDTH = 256
CV_KERNEL = 31
DL_HEADS = 8
DL_STEPS = 128
ROPE_THETA = 10000.0
X_HEADS = 4
FFN_KERNEL = 3

VMEM_LIMIT = 56 * 1024 * 1024

NT_DIMS = (((1,), (1,)), ((), ()))


def _cparams(*sem):
    return pltpu.CompilerParams(dimension_semantics=sem, vmem_limit_bytes=VMEM_LIMIT)


def _rms(x, g):
    ms = jnp.mean(x * x, axis=-1, keepdims=True)
    return x * lax.rsqrt(ms + EPS) * g


def _norm_matmul_kernel(x_ref, g_ref, w_ref, o_ref, *, n_chunk, out_scale):
    xn = _rms(x_ref[...], g_ref[...]).astype(BF16)
    n = w_ref.shape[1]
    for c in range(0, n, n_chunk):
        y = jnp.dot(xn, w_ref[:, c:c + n_chunk], preferred_element_type=F32)
        if out_scale != 1.0:
            y = y * out_scale
        o_ref[:, c:c + n_chunk] = y.astype(o_ref.dtype)


def norm_matmul(x, g, w, *, out_dtype, tm, n_chunk=256, out_scale=1.0):
    m, k = x.shape
    n = w.shape[1]
    return pl.pallas_call(
        functools.partial(_norm_matmul_kernel, n_chunk=n_chunk, out_scale=out_scale),
        grid=(m // tm,),
        in_specs=[pl.BlockSpec((tm, k), lambda i: (i, 0)),
                  pl.BlockSpec((1, k), lambda i: (0, 0)),
                  pl.BlockSpec((k, n), lambda i: (0, 0))],
        out_specs=pl.BlockSpec((tm, n), lambda i: (i, 0)),
        out_shape=jax.ShapeDtypeStruct((m, n), out_dtype),
        compiler_params=_cparams("parallel"),
        name="norm_matmul",
    )(x, g.reshape(1, k), w)


def _matmul_norm_res_kernel(*refs, n_in):
    xs = refs[:n_in]
    ws = refs[n_in:2 * n_in]
    res_ref, g_ref, o_ref = refs[2 * n_in:]
    y = jnp.dot(xs[0][...], ws[0][...], preferred_element_type=F32)
    for x_ref, w_ref in zip(xs[1:], ws[1:]):
        y = y + jnp.dot(x_ref[...], w_ref[...], preferred_element_type=F32)
    o_ref[...] = res_ref[...] + _rms(y, g_ref[...])


def matmul_norm_residual(xs, ws, res, g, *, tm):
    m, d = res.shape
    n_in = len(xs)
    in_specs = [pl.BlockSpec((tm, x.shape[1]), lambda i: (i, 0)) for x in xs]
    in_specs += [pl.BlockSpec(w.shape, lambda i: (0, 0)) for w in ws]
    in_specs += [pl.BlockSpec((tm, d), lambda i: (i, 0)), pl.BlockSpec((1, d), lambda i: (0, 0))]
    return pl.pallas_call(
        functools.partial(_matmul_norm_res_kernel, n_in=n_in),
        grid=(m // tm,),
        in_specs=in_specs,
        out_specs=pl.BlockSpec((tm, d), lambda i: (i, 0)),
        out_shape=jax.ShapeDtypeStruct((m, d), F32),
        compiler_params=_cparams("parallel"),
        name="matmul_norm_residual",
    )(*xs, *ws, res, g.reshape(1, d))


SB_MASKED_Z = -1e4
SB_DEAD = -104.0


def _sb_kernel(q_ref, k_ref, v_ref, o_ref, kt_ref, vb_ref, *, seq, tq, n_fast):
    scale = HEAD_DIM ** -0.5
    vb_ref[...] = v_ref[...].astype(BF16)

    def transpose_keys(c, _):
        c0 = pl.multiple_of(c * LANES, LANES)
        kt_ref[:, pl.ds(c0, LANES)] = k_ref[pl.ds(c0, LANES), :].T.astype(BF16)
        return 0

    lax.fori_loop(0, seq // LANES, transpose_keys, 0)

    head0 = lax.broadcasted_iota(jnp.int32, (tq, LANES), 1) < HEAD_DIM
    row = lax.broadcasted_iota(jnp.int32, (2 * tq, tq), 0) % tq
    col = lax.broadcasted_iota(jnp.int32, (2 * tq, tq), 1)
    before = col < row
    krow = lax.broadcasted_iota(jnp.int32, (tq, tq), 0)
    kcol = lax.broadcasted_iota(jnp.int32, (tq, tq), 1)
    later = jnp.where(krow > kcol, 1.0, 0.0).astype(BF16)
    later2 = jnp.concatenate([later, later], axis=0)

    def tile(q2, kt_blk, vblk, carry, ok):
        z = jnp.dot(q2, kt_blk, preferred_element_type=F32)
        if ok is not None:
            z = jnp.where(ok, z, SB_MASKED_Z)
        softplus = jnp.log(1.0 + jnp.exp(-jnp.abs(z)))
        log_beta = jnp.minimum(z, 0.0) - softplus
        log_keep = log_beta - z
        hi = log_keep.astype(BF16)
        lo = (log_keep - hi.astype(F32)).astype(BF16)
        between = jnp.dot(jnp.concatenate([hi, lo], axis=1), later2,
                          preferred_element_type=F32) + carry
        a = jnp.exp(log_beta + between)
        pv = jnp.dot(a.astype(BF16), vblk, preferred_element_type=F32)
        return pv, carry + jnp.sum(log_keep, axis=1, keepdims=True)

    def q_body(qb, _):
        q0 = pl.multiple_of(qb * tq, tq)
        q = q_ref[pl.ds(q0, tq), :] * scale
        q2 = jnp.concatenate([jnp.where(head0, q, 0.0), jnp.where(head0, 0.0, q)], axis=0).astype(BF16)
        acc = jnp.zeros((2 * tq, LANES), F32)
        carry = jnp.zeros((2 * tq, 1), F32)
        for j in range(n_fast):
            kblock = qb - j
            k0 = pl.multiple_of(jnp.maximum(kblock, 0) * tq, tq)
            ok = before if j == 0 else kblock >= 0
            pv, carry = tile(q2, kt_ref[:, pl.ds(k0, tq)], vb_ref[pl.ds(k0, tq), :], carry, ok)
            acc = acc + pv

        def alive(st):
            kblock, _, c = st
            return (kblock >= 0) & (jnp.max(c) > SB_DEAD)

        def k_body(st):
            kblock, a, c = st
            k0 = pl.multiple_of(kblock * tq, tq)
            pv, c = tile(q2, kt_ref[:, pl.ds(k0, tq)], vb_ref[pl.ds(k0, tq), :], c, None)
            return kblock - 1, a + pv, c

        _, acc, _ = lax.while_loop(alive, k_body, (qb - n_fast, acc, carry))
        o_ref[pl.ds(q0, tq), :] = jnp.where(head0, acc[:tq], acc[tq:]).astype(o_ref.dtype)
        return 0

    lax.fori_loop(0, seq // tq, q_body, 0)


def stick_breaking(u, *, tq=256, n_fast=2):
    b, s, _ = u.shape
    pairs = SB_HEADS // 2
    blk = lambda off: pl.BlockSpec((None, s, LANES), lambda i, p: (i, 0, off + p))
    return pl.pallas_call(
        functools.partial(_sb_kernel, seq=s, tq=tq, n_fast=n_fast),
        grid=(b, pairs),
        in_specs=[blk(0), blk(pairs), blk(2 * pairs)],
        out_specs=pl.BlockSpec((None, s, LANES), lambda i, p: (i, 0, p)),
        out_shape=jax.ShapeDtypeStruct((b, s, SB_HEADS * HEAD_DIM), BF16),
        scratch_shapes=[pltpu.VMEM((LANES, s), BF16), pltpu.VMEM((s, LANES), BF16)],
        compiler_params=_cparams("parallel", "parallel"),
        name="stick_breaking",
    )(u, u, u)


CV_HALO = 32


def _conv_kernel(val_ref, gate_ref, hval_ref, hgate_ref, w_ref, b_ref, lng_ref, lnb_ref,
                 pw_ref, pwb_ref, o_ref, g_scr, *, tm):
    first = pl.program_id(1) == 0
    g_scr[CV_HALO:, :] = val_ref[...] * jax.nn.sigmoid(gate_ref[...])
    halo = hval_ref[...] * jax.nn.sigmoid(hgate_ref[...])
    g_scr[:CV_HALO, :] = jnp.where(first, 0.0, halo)
    w = w_ref[...]
    g_ext = g_scr[...]
    acc = jnp.zeros((tm, CV_WIDTH), F32) + b_ref[...]
    rolled = {}
    for k in range(CV_KERNEL):
        lo = CV_HALO - (CV_KERNEL - 1) + k
        start = -(-lo // 8) * 8
        shift = start - lo
        if shift not in rolled:
            rolled[shift] = pltpu.roll(g_ext, shift, axis=0) if shift else g_ext
        acc = acc + rolled[shift][start:start + tm, :] * w[k:k + 1, :]
    mu = jnp.mean(acc, axis=-1, keepdims=True)
    xc = acc - mu
    var = jnp.mean(xc * xc, axis=-1, keepdims=True)
    y = xc * lax.rsqrt(var + EPS) * lng_ref[...] + lnb_ref[...]
    y = y * jax.nn.sigmoid(y)
    out = jnp.dot(y.astype(BF16), pw_ref[...], preferred_element_type=F32) + pwb_ref[...]
    o_ref[...] = out.astype(o_ref.dtype)


def conformer_conv(u, cv_w, cv_b, ln_g, ln_b, pw_w, pw_b, *, tm=512):
    b, s, _ = u.shape
    c = CV_WIDTH
    val_blk = 3 * SB_HEADS * HEAD_DIM // c
    hpt = tm // CV_HALO
    cur = lambda off: pl.BlockSpec((None, tm, c), lambda i, t: (i, t, off))
    halo = lambda off: pl.BlockSpec((None, CV_HALO, c),
                                    lambda i, t: (i, jnp.maximum(t * hpt - 1, 0), off))
    vec = pl.BlockSpec((1, c), lambda i, t: (0, 0))
    return pl.pallas_call(
        functools.partial(_conv_kernel, tm=tm),
        grid=(b, s // tm),
        in_specs=[cur(val_blk), cur(val_blk + 1), halo(val_blk), halo(val_blk + 1),
                  pl.BlockSpec((CV_KERNEL, c), lambda i, t: (0, 0)), vec, vec, vec,
                  pl.BlockSpec((c, c), lambda i, t: (0, 0)), vec],
        out_specs=pl.BlockSpec((None, tm, c), lambda i, t: (i, t, 0)),
        out_shape=jax.ShapeDtypeStruct((b, s, c), BF16),
        scratch_shapes=[pltpu.VMEM((tm + CV_HALO, c), F32)],
        compiler_params=_cparams("parallel", "parallel"),
        name="conformer_conv",
    )(u, u, u, u, cv_w, cv_b.reshape(1, c), ln_g.reshape(1, c), ln_b.reshape(1, c),
      pw_w, pw_b.reshape(1, c))


def _rope_table_kernel(pos_ref, invf_ref, cos_ref, sin_ref, *, seq):
    half = HEAD_DIM // 2
    ang = invf_ref[...] * pos_ref[...].astype(F32)
    c = jnp.cos(ang)
    s = jnp.sin(ang)
    cos_t = jnp.concatenate([c, c, c, c], axis=0)
    sin_t = jnp.concatenate([-s, s, -s, s], axis=0)
    del half
    for j in range(seq // LANES):
        cos_ref[j * LANES:(j + 1) * LANES, :] = cos_t[:, j * LANES:(j + 1) * LANES].T
        sin_ref[j * LANES:(j + 1) * LANES, :] = sin_t[:, j * LANES:(j + 1) * LANES].T


def rope_tables(positions):
    b, s = positions.shape
    half = HEAD_DIM // 2
    inv_freq = ROPE_THETA ** (-jnp.arange(half, dtype=F32) / half)
    out = jax.ShapeDtypeStruct((b, s, LANES), F32)
    return pl.pallas_call(
        functools.partial(_rope_table_kernel, seq=s),
        grid=(b,),
        in_specs=[pl.BlockSpec((None, 1, s), lambda i: (i, 0, 0)),
                  pl.BlockSpec((half, 1), lambda i: (0, 0))],
        out_specs=[pl.BlockSpec((None, s, LANES), lambda i: (i, 0, 0))] * 2,
        out_shape=[out, out],
        compiler_params=_cparams("parallel"),
        name="rope_tables",
    )(positions.reshape(b, 1, s), inv_freq.reshape(half, 1))


DL_DILATIONS = (1, 4, 16)
DL_GROUP = 8
DL_PAD = DL_STEPS


def _dil_kernel(q_ref, k_ref, v_ref, cos_ref, sin_ref, o_ref,
                qr, kr, q0s, q1s, kts, vs, accd, md, ld, acct, mt, lt, *, seq):
    t = DL_STEPS
    scale = HEAD_DIM ** -0.5
    sub = seq // 4
    head0 = lax.broadcasted_iota(jnp.int32, (t, LANES), 1) < HEAD_DIM
    qi = lax.broadcasted_iota(jnp.int32, (2 * t, 2 * t), 0) % t
    kj = lax.broadcasted_iota(jnp.int32, (2 * t, 2 * t), 1)
    band = (kj >= qi) & (kj <= qi + t)
    band_first = band & (kj >= t)

    def valid_mask(n):
        return band & (kj >= jnp.where(n == 0, t, 0))

    def rope(x, cos, sin):
        ln = lax.broadcasted_iota(jnp.int32, x.shape, 1)
        first_half = (ln % HEAD_DIM) < (HEAD_DIM // 2)
        partner = jnp.where(first_half, pltpu.roll(x, LANES - HEAD_DIM // 2, axis=1),
                            pltpu.roll(x, HEAD_DIM // 2, axis=1))
        return x * cos + partner * sin

    def rope_body(c, _):
        rows = pl.ds(pl.multiple_of(c * t, t), t)
        cos = cos_ref[rows, :]
        sin = sin_ref[rows, :]
        qr[rows, :] = rope(q_ref[rows, :], cos, sin) * scale
        kr[rows, :] = rope(k_ref[rows, :], cos, sin)
        return 0

    lax.fori_loop(0, seq // t, rope_body, 0, unroll=4)

    for li, d in enumerate(DL_DILATIONS):
        kts[li, :, :DL_PAD] = jnp.zeros((LANES, DL_PAD), BF16)
        vs[li, :DL_PAD, :] = jnp.zeros((DL_PAD, LANES), BF16)
        per_res = seq // d // t

        def layout_body(c, _, li=li, d=d, per_res=per_res):
            res = c // per_res
            j0 = (c % per_res) * t
            src = pl.ds(res + d * j0, t, stride=d) if d > 1 else pl.ds(pl.multiple_of(c * t, t), t)
            dst0 = pl.multiple_of(c * t, t)
            q = qr[src, :]
            q0s[li, pl.ds(dst0, t), :] = jnp.where(head0, q, 0.0).astype(BF16)
            q1s[li, pl.ds(dst0, t), :] = jnp.where(head0, 0.0, q).astype(BF16)
            kts[li, :, pl.ds(DL_PAD + dst0, t)] = kr[src, :].T.astype(BF16)
            vs[li, pl.ds(DL_PAD + dst0, t), :] = v_ref[src, :].astype(BF16)
            return 0

        lax.fori_loop(0, seq // t, layout_body, 0, unroll=4)

    def tile(li, r0, valid, prev):
        q2 = jnp.concatenate([q0s[li, pl.ds(r0, t), :], q1s[li, pl.ds(r0, t), :]], axis=0)
        kt = kts[li, :, pl.ds(r0, 2 * t)]
        vw = vs[li, pl.ds(r0, 2 * t), :]
        s = jnp.where(valid, jnp.dot(q2, kt, preferred_element_type=F32), -jnp.inf)
        m2 = jnp.max(s, axis=1, keepdims=True)
        p = jnp.exp(s - m2)
        l2 = jnp.sum(p, axis=1, keepdims=True)
        pv = jnp.dot(p.astype(BF16), vw, preferred_element_type=F32)
        m_new = jnp.where(head0, m2[:t], m2[t:])
        l_new = jnp.where(head0, l2[:t], l2[t:])
        acc_new = jnp.where(head0, pv[:t], pv[t:])
        if prev is not None:
            m_prev, l_prev, acc_prev = prev
            m_tile = m_new
            m_new = jnp.maximum(m_prev, m_tile)
            a_prev = jnp.exp(m_prev - m_new)
            a_tile = jnp.exp(m_tile - m_new)
            l_new = a_prev * l_prev + a_tile * l_new
            acc_new = a_prev * acc_prev + a_tile * acc_new
        return m_new, l_new, acc_new

    grp = DL_GROUP
    n_trips = seq // t // grp

    def d4_group(i, _):
        n0 = (i * grp) % (sub // t)
        outs = []
        for g in range(grp):
            r0 = pl.multiple_of((i * grp + g) * t, t)
            outs.append((r0, tile(1, r0, valid_mask(n0) if g == 0 else band, None)))
        for r0, (m_new, l_new, acc_new) in outs:
            md[pl.ds(r0, t), :] = m_new
            ld[pl.ds(r0, t), :] = l_new
            accd[pl.ds(r0, t), :] = acc_new
        return 0

    lax.fori_loop(0, n_trips, d4_group, 0)

    nb16 = seq // 16 // t

    def d16_group(i, _):
        outs = []
        for g in range(grp):
            idx = i * grp + g
            res16 = idx // nb16
            n = g % nb16
            rows = pl.ds((res16 % 4) * sub + res16 // 4 + 4 * t * n, t, stride=4)
            prev = (md[rows, :], ld[rows, :], accd[rows, :])
            r0 = pl.multiple_of(idx * t, t)
            outs.append((rows, tile(2, r0, band_first if n == 0 else band, prev)))
        for rows, (m_new, l_new, acc_new) in outs:
            md[rows, :] = m_new
            ld[rows, :] = l_new
            accd[rows, :] = acc_new
        return 0

    lax.fori_loop(0, n_trips, d16_group, 0)

    def d1_group(i, _):
        outs = []
        for g in range(grp):
            n = i * grp + g
            r0 = pl.multiple_of(n * t, t)
            for r4 in range(4):
                src = pl.ds(r4 * sub + n * (t // 4), t // 4)
                dst = pl.ds(g * t + r4, t // 4, stride=4)
                mt[dst, :] = md[src, :]
                lt[dst, :] = ld[src, :]
                acct[dst, :] = accd[src, :]
            tmp = pl.ds(g * t, t)
            _, l_new, acc_new = tile(0, r0, valid_mask(n) if g == 0 else band,
                                     (mt[tmp, :], lt[tmp, :], acct[tmp, :]))
            outs.append((r0, acc_new / l_new))
        for r0, out in outs:
            o_ref[pl.ds(r0, t), :] = out.astype(o_ref.dtype)
        return 0

    lax.fori_loop(0, n_trips, d1_group, 0)


def dilated_attention(u, cos_t, sin_t):
    b, s, _ = u.shape
    assert s % (16 * DL_STEPS) == 0 and (s // 4 // DL_STEPS) % DL_GROUP == 0
    assert DL_GROUP % (s // 16 // DL_STEPS) == 0
    pairs = DL_HEADS // 2
    first = (3 * SB_HEADS * HEAD_DIM + 2 * CV_WIDTH) // LANES
    blk = lambda off: pl.BlockSpec((None, s, LANES), lambda i, p: (i, 0, off + p))
    tab = pl.BlockSpec((None, s, LANES), lambda i, p: (i, 0, 0))
    nl = len(DL_DILATIONS)
    stat = pltpu.VMEM((s, LANES), F32)
    small = pltpu.VMEM((DL_GROUP * DL_STEPS, LANES), F32)
    return pl.pallas_call(
        functools.partial(_dil_kernel, seq=s),
        grid=(b, pairs),
        in_specs=[blk(first), blk(first + pairs), blk(first + 2 * pairs), tab, tab],
        out_specs=pl.BlockSpec((None, s, LANES), lambda i, p: (i, 0, p)),
        out_shape=jax.ShapeDtypeStruct((b, s, DL_HEADS * HEAD_DIM), BF16),
        scratch_shapes=[stat, stat, pltpu.VMEM((nl, s, LANES), BF16), pltpu.VMEM((nl, s, LANES), BF16),
                        pltpu.VMEM((nl, LANES, DL_PAD + s), BF16), pltpu.VMEM((nl, DL_PAD + s, LANES), BF16),
                        stat, stat, stat, small, small, small],
        compiler_params=_cparams("parallel", "parallel"),
        name="dilated_attention",
    )(u, u, u, cos_t, sin_t)


def _xattn_kernel(q_ref, kv_ref, o_ref, *, d_model):
    hd = d_model // X_HEADS
    for h in range(X_HEADS):
        q = q_ref[:, h * hd:(h + 1) * hd]
        k = kv_ref[:, h * hd:(h + 1) * hd]
        v = kv_ref[:, d_model + h * hd:d_model + (h + 1) * hd]
        s = lax.dot_general(q, k, NT_DIMS, preferred_element_type=F32)
        p = jnp.exp(s - jnp.max(s, axis=1, keepdims=True))
        den = jnp.sum(p, axis=1, keepdims=True)
        o = jnp.dot(p.astype(BF16), v, preferred_element_type=F32) / den
        o_ref[:, h * hd:(h + 1) * hd] = o.astype(o_ref.dtype)


def cross_attention(q, kv, *, tm=512):
    b, s, d = q.shape
    n_mem = kv.shape[1]
    return pl.pallas_call(
        functools.partial(_xattn_kernel, d_model=d),
        grid=(b, s // tm),
        in_specs=[pl.BlockSpec((None, tm, d), lambda i, t: (i, t, 0)),
                  pl.BlockSpec((None, n_mem, 2 * d), lambda i, t: (i, 0, 0))],
        out_specs=pl.BlockSpec((None, tm, d), lambda i, t: (i, t, 0)),
        out_shape=jax.ShapeDtypeStruct((b, s, d), BF16),
        compiler_params=_cparams("parallel", "parallel"),
        name="cross_attention",
    )(q, kv)


FFN_HALO = 8


def _ffn_kernel(h_ref, halo_ref, gpre_ref, wup_ref, cw_ref, cb_ref, wdn_ref, gpost_ref, o_ref, act_ref,
                *, tm, d_ff, n_chunk):
    first = pl.program_id(1) == 0
    h = h_ref[...]
    g = gpre_ref[...]
    halo = jnp.where(first, 0.0, _rms(halo_ref[...], g))
    xn = jnp.concatenate([halo, _rms(h, g)], axis=0).astype(BF16)

    def conv(u, c0):
        y = cb_ref[:, c0:c0 + n_chunk] + u[FFN_HALO:, :] * cw_ref[FFN_KERNEL - 1:FFN_KERNEL, c0:c0 + n_chunk]
        for k in range(FFN_KERNEL - 1):
            past = pltpu.roll(u, FFN_KERNEL - 1 - k, axis=0)[FFN_HALO:, :]
            y = y + past * cw_ref[k:k + 1, c0:c0 + n_chunk]
        return y

    for c0 in range(0, d_ff, n_chunk):
        ug = jnp.dot(xn, wup_ref[:, c0:c0 + n_chunk], preferred_element_type=F32)
        uv = jnp.dot(xn, wup_ref[:, d_ff + c0:d_ff + c0 + n_chunk], preferred_element_type=F32)
        act = jax.nn.gelu(conv(ug, c0), approximate=True) * conv(uv, d_ff + c0)
        act_ref[:, c0:c0 + n_chunk] = act.astype(BF16)
    y = jnp.dot(act_ref[...], wdn_ref[...], preferred_element_type=F32)
    o_ref[...] = h + _rms(y, gpost_ref[...])


def conv_ffn(h, g_pre, w_up, conv_w, conv_b, w_down, g_post, *, tm=512, n_chunk=256):
    b, s, d = h.shape
    d_ff = w_down.shape[0]
    hpt = tm // FFN_HALO
    const = lambda shape: pl.BlockSpec(shape, lambda i, t: (0, 0))
    return pl.pallas_call(
        functools.partial(_ffn_kernel, tm=tm, d_ff=d_ff, n_chunk=n_chunk),
        grid=(b, s // tm),
        in_specs=[pl.BlockSpec((None, tm, d), lambda i, t: (i, t, 0)),
                  pl.BlockSpec((None, FFN_HALO, d), lambda i, t: (i, jnp.maximum(t * hpt - 1, 0), 0)),
                  const((1, d)), const((d, 2 * d_ff)), const((FFN_KERNEL, 2 * d_ff)),
                  const((1, 2 * d_ff)), const((d_ff, d)), const((1, d))],
        out_specs=pl.BlockSpec((None, tm, d), lambda i, t: (i, t, 0)),
        out_shape=jax.ShapeDtypeStruct((b, s, d), F32),
        scratch_shapes=[pltpu.VMEM((tm, d_ff), BF16)],
        compiler_params=_cparams("parallel", "parallel"),
        name="conv_ffn",
    )(h, h, g_pre.reshape(1, d), w_up, conv_w, conv_b.reshape(1, 2 * d_ff), w_down,
      g_post.reshape(1, d))


def kernel(x, mem, positions, mix_norm_pre, w_in, cv_w, cv_b, cv_ln_g, cv_ln_b, cv_pw_w, cv_pw_b,
           w_out, mix_norm_post, x_norm_pre, mem_norm, x_wq, x_wk, x_wv, x_wo, x_norm_post,
           ffn_norm_pre, ffn_w_up, ffn_conv_w, ffn_conv_b, ffn_w_down, ffn_norm_post):
    b, s, d = x.shape
    n_mem = mem.shape[1]
    depth = w_in.shape[0]
    m = b * s
    sb_w = SB_HEADS * HEAD_DIM
    tm = 512

    cos_t, sin_t = rope_tables(positions)
    mem2 = mem.reshape(b * n_mem, d)
    h = x
    for l in range(depth):
        u = norm_matmul(h.reshape(m, d), mix_norm_pre[l], w_in[l].astype(BF16),
                        out_dtype=F32, tm=tm).reshape(b, s, -1)
        a_out = stick_breaking(u)
        b_out = conformer_conv(u, cv_w[l], cv_b[l], cv_ln_g[l], cv_ln_b[l],
                               cv_pw_w[l].astype(BF16), cv_pw_b[l])
        c_out = dilated_attention(u, cos_t, sin_t)
        wo = w_out[l].astype(BF16)
        h = matmul_norm_residual(
            [a_out.reshape(m, -1), b_out.reshape(m, -1), c_out.reshape(m, -1)],
            [wo[:sb_w], wo[sb_w:sb_w + CV_WIDTH], wo[sb_w + CV_WIDTH:]],
            h.reshape(m, d), mix_norm_post[l], tm=tm)

        q = norm_matmul(h, x_norm_pre[l], x_wq[l].astype(BF16), out_dtype=BF16, tm=tm,
                        out_scale=(d // X_HEADS) ** -0.5).reshape(b, s, d)
        wkv = jnp.concatenate([x_wk[l], x_wv[l]], axis=1).astype(BF16)
        kv = norm_matmul(mem2, mem_norm[l], wkv, out_dtype=BF16, tm=tm).reshape(b, n_mem, 2 * d)
        o = cross_attention(q, kv)
        h = matmul_norm_residual([o.reshape(m, d)], [x_wo[l].astype(BF16)], h, x_norm_post[l], tm=tm)

        h = conv_ffn(h.reshape(b, s, d), ffn_norm_pre[l], ffn_w_up[l].astype(BF16), ffn_conv_w[l],
                     ffn_conv_b[l], ffn_w_down[l].astype(BF16), ffn_norm_post[l])
    return h
```

```python
import functools

import jax
import jax.numpy as jnp
from jax import lax
from jax.experimental import pallas as pl
from jax.experimental.pallas import tpu as pltpu

F32 = jnp.float32
BF16 = jnp.bfloat16

EPS = 1e-6
HEAD_DIM = 64
LANES = 128
SB_HEADS = 4
CV_WIDTH = 256
CV_KERNEL = 31
DL_HEADS = 8
DL_STEPS = 128
ROPE_THETA = 10000.0
X_HEADS = 4
FFN_KERNEL = 3

VMEM_LIMIT = 56 * 1024 * 1024

NT_DIMS = (((1,), (1,)), ((), ()))


def _cparams(*sem):
    return pltpu.CompilerParams(dimension_semantics=sem, vmem_limit_bytes=VMEM_LIMIT)


def _rms(x, g):
    ms = jnp.mean(x * x, axis=-1, keepdims=True)
    return x * lax.rsqrt(ms + EPS) * g


def _norm_matmul_kernel(x_ref, g_ref, w_ref, o_ref, *, n_chunk):
    xn = _rms(x_ref[...], g_ref[...]).astype(BF16)
    n = w_ref.shape[1]
    for c in range(0, n, n_chunk):
        y = jnp.dot(xn, w_ref[:, c:c + n_chunk], preferred_element_type=F32)
        o_ref[:, c:c + n_chunk] = y.astype(o_ref.dtype)


def norm_matmul(x, g, w, *, out_dtype, tm, n_chunk=256):
    m, k = x.shape
    n = w.shape[1]
    return pl.pallas_call(
        functools.partial(_norm_matmul_kernel, n_chunk=n_chunk),
        grid=(m // tm,),
        in_specs=[pl.BlockSpec((tm, k), lambda i: (i, 0)),
                  pl.BlockSpec((1, k), lambda i: (0, 0)),
                  pl.BlockSpec((k, n), lambda i: (0, 0))],
        out_specs=pl.BlockSpec((tm, n), lambda i: (i, 0)),
        out_shape=jax.ShapeDtypeStruct((m, n), out_dtype),
        compiler_params=_cparams("parallel"),
        name="norm_matmul",
    )(x, g.reshape(1, k), w)


DL_FIRST_COL = 3 * SB_HEADS * HEAD_DIM + 2 * CV_WIDTH
DL_WIDTH = DL_HEADS * HEAD_DIM


def _rope(x, cos, sin):
    ln = lax.broadcasted_iota(jnp.int32, x.shape, 1)
    first_half = (ln % HEAD_DIM) < (HEAD_DIM // 2)
    partner = jnp.where(first_half, pltpu.roll(x, LANES - HEAD_DIM // 2, axis=1),
                        pltpu.roll(x, HEAD_DIM // 2, axis=1))
    return x * cos + partner * sin


def _in_proj_kernel(x_ref, g_ref, w_ref, cos_ref, sin_ref, o_ref, *, n_chunk):
    xn = _rms(x_ref[...], g_ref[...]).astype(BF16)
    cos = cos_ref[...]
    sin = sin_ref[...]
    k_col = DL_FIRST_COL + DL_WIDTH
    v_col = k_col + DL_WIDTH
    for c in range(0, w_ref.shape[1], n_chunk):
        y = jnp.dot(xn, w_ref[:, c:c + n_chunk], preferred_element_type=F32)
        for col in range(c, c + n_chunk, LANES):
            blk = y[:, col - c:col - c + LANES]
            if DL_FIRST_COL <= col < k_col:
                blk = _rope(blk, cos, sin) * HEAD_DIM ** -0.5
            elif k_col <= col < v_col:
                blk = _rope(blk, cos, sin)
            o_ref[:, col:col + LANES] = blk


def mixer_in_proj(x, g, w, cos_t, sin_t, *, tm, n_chunk=256):
    m, k = x.shape
    n = w.shape[1]
    return pl.pallas_call(
        functools.partial(_in_proj_kernel, n_chunk=n_chunk),
        grid=(m // tm,),
        in_specs=[pl.BlockSpec((tm, k), lambda i: (i, 0)),
                  pl.BlockSpec((1, k), lambda i: (0, 0)),
                  pl.BlockSpec((k, n), lambda i: (0, 0)),
                  pl.BlockSpec((tm, LANES), lambda i: (i, 0)),
                  pl.BlockSpec((tm, LANES), lambda i: (i, 0))],
        out_specs=pl.BlockSpec((tm, n), lambda i: (i, 0)),
        out_shape=jax.ShapeDtypeStruct((m, n), F32),
        compiler_params=_cparams("parallel"),
        name="mixer_in_proj",
    )(x, g.reshape(1, k), w, cos_t, sin_t)


SB_MASKED_Z = -1e4
SB_DEAD = -104.0


def _sb_kernel(q_ref, k_ref, v_ref, o_ref, kt_ref, vb_ref, *, seq, tq, n_fast, q_group):
    scale = HEAD_DIM ** -0.5
    vb_ref[...] = v_ref[...].astype(BF16)

    def transpose_keys(c, _):
        c0 = pl.multiple_of(c * LANES, LANES)
        kt_ref[:, pl.ds(c0, LANES)] = k_ref[pl.ds(c0, LANES), :].T.astype(BF16)
        return 0

    lax.fori_loop(0, seq // LANES, transpose_keys, 0)

    head0 = lax.broadcasted_iota(jnp.int32, (tq, LANES), 1) < HEAD_DIM
    row = lax.broadcasted_iota(jnp.int32, (2 * tq, tq), 0) % tq
    col = lax.broadcasted_iota(jnp.int32, (2 * tq, tq), 1)
    before = col < row
    krow = lax.broadcasted_iota(jnp.int32, (tq, tq), 0)
    kcol = lax.broadcasted_iota(jnp.int32, (tq, tq), 1)
    later = jnp.where(krow > kcol, 1.0, 0.0).astype(BF16)
    later2 = jnp.concatenate([later, later], axis=0)

    def tile(q2, kt_blk, vblk, carry, ok):
        z = jnp.dot(q2, kt_blk, preferred_element_type=F32)
        if ok is not None:
            z = jnp.where(ok, z, SB_MASKED_Z)
        softplus = jnp.log(1.0 + jnp.exp(-jnp.abs(z)))
        log_beta = jnp.minimum(z, 0.0) - softplus
        log_keep = log_beta - z
        hi = log_keep.astype(BF16)
        lo = (log_keep - hi.astype(F32)).astype(BF16)
        between = jnp.dot(jnp.concatenate([hi, lo], axis=1), later2,
                          preferred_element_type=F32) + carry
        a = jnp.exp(log_beta + between)
        pv = jnp.dot(a.astype(BF16), vblk, preferred_element_type=F32)
        return pv, carry + jnp.sum(log_keep, axis=1, keepdims=True)

    def near_blocks(qb):
        q0 = pl.multiple_of(qb * tq, tq)
        q = q_ref[pl.ds(q0, tq), :] * scale
        q2 = jnp.concatenate([jnp.where(head0, q, 0.0), jnp.where(head0, 0.0, q)], axis=0).astype(BF16)
        acc = jnp.zeros((2 * tq, LANES), F32)
        carry = jnp.zeros((2 * tq, 1), F32)
        for j in range(n_fast):
            kblock = qb - j
            k0 = pl.multiple_of(jnp.maximum(kblock, 0) * tq, tq)
            ok = before if j == 0 else kblock >= 0
            pv, carry = tile(q2, kt_ref[:, pl.ds(k0, tq)], vb_ref[pl.ds(k0, tq), :], carry, ok)
            acc = acc + pv
        return q2, acc, carry

    def far_blocks(qb, q2, acc, carry):
        def alive(st):
            kblock, _, c = st
            return (kblock >= 0) & (jnp.max(c) > SB_DEAD)

        def k_body(st):
            kblock, a, c = st
            k0 = pl.multiple_of(kblock * tq, tq)
            pv, c = tile(q2, kt_ref[:, pl.ds(k0, tq)], vb_ref[pl.ds(k0, tq), :], c, None)
            return kblock - 1, a + pv, c

        return lax.while_loop(alive, k_body, (qb - n_fast, acc, carry))[1]

    def q_body(i, _):
        started = [(i * q_group + g,) + near_blocks(i * q_group + g) for g in range(q_group)]
        for qb, q2, acc, carry in started:
            acc = far_blocks(qb, q2, acc, carry)
            q0 = pl.multiple_of(qb * tq, tq)
            o_ref[pl.ds(q0, tq), :] = jnp.where(head0, acc[:tq], acc[tq:]).astype(o_ref.dtype)
        return 0

    lax.fori_loop(0, seq // tq // q_group, q_body, 0)


def stick_breaking(u, *, tq=256, n_fast=2, q_group=2):
    b, s, _ = u.shape
    pairs = SB_HEADS // 2
    blk = lambda off: pl.BlockSpec((None, s, LANES), lambda i, p: (i, 0, off + p))
    return pl.pallas_call(
        functools.partial(_sb_kernel, seq=s, tq=tq, n_fast=n_fast, q_group=q_group),
        grid=(b, pairs),
        in_specs=[blk(0), blk(pairs), blk(2 * pairs)],
        out_specs=pl.BlockSpec((None, s, LANES), lambda i, p: (i, 0, p)),
        out_shape=jax.ShapeDtypeStruct((b, s, SB_HEADS * HEAD_DIM), BF16),
        scratch_shapes=[pltpu.VMEM((LANES, s), BF16), pltpu.VMEM((s, LANES), BF16)],
        compiler_params=_cparams("parallel", "parallel"),
        name="stick_breaking",
    )(u, u, u)


CV_HALO = 32


def _conv_kernel(val_ref, gate_ref, hval_ref, hgate_ref, w_ref, b_ref, lng_ref, lnb_ref,
                 pw_ref, pwb_ref, o_ref, g_scr, *, tm):
    first = pl.program_id(1) == 0
    g_scr[CV_HALO:, :] = val_ref[...] * jax.nn.sigmoid(gate_ref[...])
    halo = hval_ref[...] * jax.nn.sigmoid(hgate_ref[...])
    g_scr[:CV_HALO, :] = jnp.where(first, 0.0, halo)
    w = w_ref[...]
    g_ext = g_scr[...]
    acc = jnp.zeros((tm, CV_WIDTH), F32) + b_ref[...]
    rolled = {}
    for k in range(CV_KERNEL):
        lo = CV_HALO - (CV_KERNEL - 1) + k
        start = -(-lo // 8) * 8
        shift = start - lo
        if shift not in rolled:
            rolled[shift] = pltpu.roll(g_ext, shift, axis=0) if shift else g_ext
        acc = acc + rolled[shift][start:start + tm, :] * w[k:k + 1, :]
    mu = jnp.mean(acc, axis=-1, keepdims=True)
    xc = acc - mu
    var = jnp.mean(xc * xc, axis=-1, keepdims=True)
    y = xc * lax.rsqrt(var + EPS) * lng_ref[...] + lnb_ref[...]
    y = y * jax.nn.sigmoid(y)
    out = jnp.dot(y.astype(BF16), pw_ref[...], preferred_element_type=F32) + pwb_ref[...]
    o_ref[...] = out.astype(o_ref.dtype)


def conformer_conv(u, cv_w, cv_b, ln_g, ln_b, pw_w, pw_b, *, tm=512):
    b, s, _ = u.shape
    c = CV_WIDTH
    val_blk = 3 * SB_HEADS * HEAD_DIM // c
    hpt = tm // CV_HALO
    cur = lambda off: pl.BlockSpec((None, tm, c), lambda i, t: (i, t, off))
    halo = lambda off: pl.BlockSpec((None, CV_HALO, c),
                                    lambda i, t: (i, jnp.maximum(t * hpt - 1, 0), off))
    vec = pl.BlockSpec((1, c), lambda i, t: (0, 0))
    return pl.pallas_call(
        functools.partial(_conv_kernel, tm=tm),
        grid=(b, s // tm),
        in_specs=[cur(val_blk), cur(val_blk + 1), halo(val_blk), halo(val_blk + 1),
                  pl.BlockSpec((CV_KERNEL, c), lambda i, t: (0, 0)), vec, vec, vec,
                  pl.BlockSpec((c, c), lambda i, t: (0, 0)), vec],
        out_specs=pl.BlockSpec((None, tm, c), lambda i, t: (i, t, 0)),
        out_shape=jax.ShapeDtypeStruct((b, s, c), BF16),
        scratch_shapes=[pltpu.VMEM((tm + CV_HALO, c), F32)],
        compiler_params=_cparams("parallel", "parallel"),
        name="conformer_conv",
    )(u, u, u, u, cv_w, cv_b.reshape(1, c), ln_g.reshape(1, c), ln_b.reshape(1, c),
      pw_w, pw_b.reshape(1, c))


def _rope_table_kernel(pos_ref, invf_ref, cos_ref, sin_ref, *, seq):
    half = HEAD_DIM // 2
    ang = invf_ref[...] * pos_ref[...].astype(F32)
    c = jnp.cos(ang)
    s = jnp.sin(ang)
    cos_t = jnp.concatenate([c, c, c, c], axis=0)
    sin_t = jnp.concatenate([-s, s, -s, s], axis=0)
    del half
    for j in range(seq // LANES):
        cos_ref[j * LANES:(j + 1) * LANES, :] = cos_t[:, j * LANES:(j + 1) * LANES].T
        sin_ref[j * LANES:(j + 1) * LANES, :] = sin_t[:, j * LANES:(j + 1) * LANES].T


def rope_tables(positions):
    b, s = positions.shape
    half = HEAD_DIM // 2
    inv_freq = ROPE_THETA ** (-jnp.arange(half, dtype=F32) / half)
    out = jax.ShapeDtypeStruct((b, s, LANES), F32)
    return pl.pallas_call(
        functools.partial(_rope_table_kernel, seq=s),
        grid=(b,),
        in_specs=[pl.BlockSpec((None, 1, s), lambda i: (i, 0, 0)),
                  pl.BlockSpec((half, 1), lambda i: (0, 0))],
        out_specs=[pl.BlockSpec((None, s, LANES), lambda i: (i, 0, 0))] * 2,
        out_shape=[out, out],
        compiler_params=_cparams("parallel"),
        name="rope_tables",
    )(positions.reshape(b, 1, s), inv_freq.reshape(half, 1))


DL_DILATIONS = (1, 4, 16)
DL_GROUP = 8
DL_PAD = DL_STEPS


def _dil_kernel(q_ref, k_ref, v_ref, o_ref, q0s, q1s, kts, vs, accd, md, ld, acct, mt, lt, *, seq):
    t = DL_STEPS
    sub = seq // 4
    head0 = lax.broadcasted_iota(jnp.int32, (t, LANES), 1) < HEAD_DIM
    qi = lax.broadcasted_iota(jnp.int32, (2 * t, 2 * t), 0) % t
    kj = lax.broadcasted_iota(jnp.int32, (2 * t, 2 * t), 1)
    band = (kj >= qi) & (kj <= qi + t)
    band_first = band & (kj >= t)

    def valid_mask(n):
        return band & (kj >= jnp.where(n == 0, t, 0))

    for li, d in enumerate(DL_DILATIONS):
        kts[li, :, :DL_PAD] = jnp.zeros((LANES, DL_PAD), BF16)
        vs[li, :DL_PAD, :] = jnp.zeros((DL_PAD, LANES), BF16)
        per_res = seq // d // t

        def layout_body(c, _, li=li, d=d, per_res=per_res):
            res = c // per_res
            j0 = (c % per_res) * t
            src = pl.ds(res + d * j0, t, stride=d) if d > 1 else pl.ds(pl.multiple_of(c * t, t), t)
            dst0 = pl.multiple_of(c * t, t)
            q = q_ref[src, :]
            q0s[li, pl.ds(dst0, t), :] = jnp.where(head0, q, 0.0).astype(BF16)
            q1s[li, pl.ds(dst0, t), :] = jnp.where(head0, 0.0, q).astype(BF16)
            kts[li, :, pl.ds(DL_PAD + dst0, t)] = k_ref[src, :].T.astype(BF16)
            vs[li, pl.ds(DL_PAD + dst0, t), :] = v_ref[src, :].astype(BF16)
            return 0

        lax.fori_loop(0, seq // t, layout_body, 0, unroll=4)

    def tile(li, r0, valid, prev):
        q2 = jnp.concatenate([q0s[li, pl.ds(r0, t), :], q1s[li, pl.ds(r0, t), :]], axis=0)
        kt = kts[li, :, pl.ds(r0, 2 * t)]
        vw = vs[li, pl.ds(r0, 2 * t), :]
        s = jnp.where(valid, jnp.dot(q2, kt, preferred_element_type=F32), -jnp.inf)
        m2 = jnp.max(s, axis=1, keepdims=True)
        p = jnp.exp(s - m2)
        l2 = jnp.sum(p, axis=1, keepdims=True)
        pv = jnp.dot(p.astype(BF16), vw, preferred_element_type=F32)
        m_new = jnp.where(head0, m2[:t], m2[t:])
        l_new = jnp.where(head0, l2[:t], l2[t:])
        acc_new = jnp.where(head0, pv[:t], pv[t:])
        if prev is not None:
            m_prev, l_prev, acc_prev = prev
            m_tile = m_new
            m_new = jnp.maximum(m_prev, m_tile)
            a_prev = jnp.exp(m_prev - m_new)
            a_tile = jnp.exp(m_tile - m_new)
            l_new = a_prev * l_prev + a_tile * l_new
            acc_new = a_prev * acc_prev + a_tile * acc_new
        return m_new, l_new, acc_new

    grp = DL_GROUP
    n_trips = seq // t // grp

    def d4_group(i, _):
        n0 = (i * grp) % (sub // t)
        outs = []
        for g in range(grp):
            r0 = pl.multiple_of((i * grp + g) * t, t)
            outs.append((r0, tile(1, r0, valid_mask(n0) if g == 0 else band, None)))
        for r0, (m_new, l_new, acc_new) in outs:
            md[pl.ds(r0, t), :] = m_new
            ld[pl.ds(r0, t), :] = l_new
            accd[pl.ds(r0, t), :] = acc_new
        return 0

    lax.fori_loop(0, n_trips, d4_group, 0)

    nb16 = seq // 16 // t

    def d16_group(i, _):
        outs = []
        for g in range(grp):
            idx = i * grp + g
            res16 = idx // nb16
            n = g % nb16
            rows = pl.ds((res16 % 4) * sub + res16 // 4 + 4 * t * n, t, stride=4)
            prev = (md[rows, :], ld[rows, :], accd[rows, :])
            r0 = pl.multiple_of(idx * t, t)
            outs.append((rows, tile(2, r0, band_first if n == 0 else band, prev)))
        for rows, (m_new, l_new, acc_new) in outs:
            md[rows, :] = m_new
            ld[rows, :] = l_new
            accd[rows, :] = acc_new
        return 0

    lax.fori_loop(0, n_trips, d16_group, 0)

    def d1_group(i, _):
        outs = []
        for g in range(grp):
            n = i * grp + g
            r0 = pl.multiple_of(n * t, t)
            for r4 in range(4):
                src = pl.ds(r4 * sub + n * (t // 4), t // 4)
                dst = pl.ds(g * t + r4, t // 4, stride=4)
                mt[dst, :] = md[src, :]
                lt[dst, :] = ld[src, :]
                acct[dst, :] = accd[src, :]
            tmp = pl.ds(g * t, t)
            _, l_new, acc_new = tile(0, r0, valid_mask(n) if g == 0 else band,
                                     (mt[tmp, :], lt[tmp, :], acct[tmp, :]))
            outs.append((r0, acc_new / l_new))
        for r0, out in outs:
            o_ref[pl.ds(r0, t), :] = out.astype(o_ref.dtype)
        return 0

    lax.fori_loop(0, n_trips, d1_group, 0)


def dilated_attention(u):
    b, s, _ = u.shape
    assert s % (16 * DL_STEPS) == 0 and (s // 4 // DL_STEPS) % DL_GROUP == 0
    assert DL_GROUP % (s // 16 // DL_STEPS) == 0
    pairs = DL_HEADS // 2
    first = DL_FIRST_COL // LANES
    blk = lambda off: pl.BlockSpec((None, s, LANES), lambda i, p: (i, 0, off + p))
    nl = len(DL_DILATIONS)
    stat = pltpu.VMEM((s, LANES), F32)
    small = pltpu.VMEM((DL_GROUP * DL_STEPS, LANES), F32)
    return pl.pallas_call(
        functools.partial(_dil_kernel, seq=s),
        grid=(b, pairs),
        in_specs=[blk(first), blk(first + pairs), blk(first + 2 * pairs)],
        out_specs=pl.BlockSpec((None, s, LANES), lambda i, p: (i, 0, p)),
        out_shape=jax.ShapeDtypeStruct((b, s, DL_HEADS * HEAD_DIM), BF16),
        scratch_shapes=[pltpu.VMEM((nl, s, LANES), BF16), pltpu.VMEM((nl, s, LANES), BF16),
                        pltpu.VMEM((nl, LANES, DL_PAD + s), BF16), pltpu.VMEM((nl, DL_PAD + s, LANES), BF16),
                        stat, stat, stat, small, small, small],
        compiler_params=_cparams("parallel", "parallel"),
        name="dilated_attention",
    )(u, u, u)


def _post_mixer_kernel(a_ref, b_ref, c_ref, h_ref, wa_ref, wb_ref, wc_ref, gmix_ref, gpre_ref, wq_ref,
                       kv_ref, wo_ref, gpost_ref, o_ref, att_ref, *, d_model):
    y = jnp.dot(a_ref[...], wa_ref[...], preferred_element_type=F32)
    y = y + jnp.dot(b_ref[...], wb_ref[...], preferred_element_type=F32)
    y = y + jnp.dot(c_ref[...], wc_ref[...], preferred_element_type=F32)
    h = h_ref[...] + _rms(y, gmix_ref[...])

    hd = d_model // X_HEADS
    qn = _rms(h, gpre_ref[...]).astype(BF16)
    q = (jnp.dot(qn, wq_ref[...], preferred_element_type=F32) * hd ** -0.5).astype(BF16)
    for hh in range(X_HEADS):
        k = kv_ref[:, hh * hd:(hh + 1) * hd]
        v = kv_ref[:, d_model + hh * hd:d_model + (hh + 1) * hd]
        s = lax.dot_general(q[:, hh * hd:(hh + 1) * hd], k, NT_DIMS, preferred_element_type=F32)
        p = jnp.exp(s - jnp.max(s, axis=1, keepdims=True))
        den = jnp.sum(p, axis=1, keepdims=True)
        att = jnp.dot(p.astype(BF16), v, preferred_element_type=F32) / den
        att_ref[:, hh * hd:(hh + 1) * hd] = att.astype(BF16)
    y = jnp.dot(att_ref[...], wo_ref[...], preferred_element_type=F32)
    o_ref[...] = h + _rms(y, gpost_ref[...])


def post_mixer(a_out, b_out, c_out, h, w_out, g_mix_post, g_x_pre, wq, kv, wo, g_x_post, *, tm=512):
    b, s, d = h.shape
    n_mem = kv.shape[1]
    wa_rows = a_out.shape[2]
    wb_rows = b_out.shape[2]
    rows = lambda w: pl.BlockSpec((None, tm, w), lambda i, t: (i, t, 0))
    const = lambda shape: pl.BlockSpec(shape, lambda i, t: (0, 0))
    vec = const((1, d))
    return pl.pallas_call(
        functools.partial(_post_mixer_kernel, d_model=d),
        grid=(b, s // tm),
        in_specs=[rows(wa_rows), rows(wb_rows), rows(c_out.shape[2]), rows(d),
                  const((wa_rows, d)), const((wb_rows, d)), const((c_out.shape[2], d)), vec, vec,
                  const((d, d)), pl.BlockSpec((None, n_mem, 2 * d), lambda i, t: (i, 0, 0)),
                  const((d, d)), vec],
        out_specs=rows(d),
        out_shape=jax.ShapeDtypeStruct((b, s, d), F32),
        scratch_shapes=[pltpu.VMEM((tm, d), BF16)],
        compiler_params=_cparams("parallel", "parallel"),
        name="post_mixer",
    )(a_out, b_out, c_out, h, w_out[:wa_rows], w_out[wa_rows:wa_rows + wb_rows],
      w_out[wa_rows + wb_rows:], g_mix_post.reshape(1, d), g_x_pre.reshape(1, d), wq, kv, wo,
      g_x_post.reshape(1, d))


FFN_HALO = 8


def _ffn_kernel(h_ref, halo_ref, gpre_ref, wup_ref, cw_ref, cb_ref, wdn_ref, gpost_ref, o_ref, act_ref,
                *, tm, d_ff, n_chunk):
    first = pl.program_id(1) == 0
    h = h_ref[...]
    g = gpre_ref[...]
    halo = jnp.where(first, 0.0, _rms(halo_ref[...], g))
    xn = jnp.concatenate([halo, _rms(h, g)], axis=0).astype(BF16)

    def conv(u, c0):
        y = cb_ref[:, c0:c0 + n_chunk] + u[FFN_HALO:, :] * cw_ref[FFN_KERNEL - 1:FFN_KERNEL, c0:c0 + n_chunk]
        for k in range(FFN_KERNEL - 1):
            past = pltpu.roll(u, FFN_KERNEL - 1 - k, axis=0)[FFN_HALO:, :]
            y = y + past * cw_ref[k:k + 1, c0:c0 + n_chunk]
        return y

    for c0 in range(0, d_ff, n_chunk):
        ug = jnp.dot(xn, wup_ref[:, c0:c0 + n_chunk], preferred_element_type=F32)
        uv = jnp.dot(xn, wup_ref[:, d_ff + c0:d_ff + c0 + n_chunk], preferred_element_type=F32)
        act = jax.nn.gelu(conv(ug, c0), approximate=True) * conv(uv, d_ff + c0)
        act_ref[:, c0:c0 + n_chunk] = act.astype(BF16)
    y = jnp.dot(act_ref[...], wdn_ref[...], preferred_element_type=F32)
    o_ref[...] = h + _rms(y, gpost_ref[...])


def conv_ffn(h, g_pre, w_up, conv_w, conv_b, w_down, g_post, *, tm=512, n_chunk=256):
    b, s, d = h.shape
    d_ff = w_down.shape[0]
    hpt = tm // FFN_HALO
    const = lambda shape: pl.BlockSpec(shape, lambda i, t: (0, 0))
    return pl.pallas_call(
        functools.partial(_ffn_kernel, tm=tm, d_ff=d_ff, n_chunk=n_chunk),
        grid=(b, s // tm),
        in_specs=[pl.BlockSpec((None, tm, d), lambda i, t: (i, t, 0)),
                  pl.BlockSpec((None, FFN_HALO, d), lambda i, t: (i, jnp.maximum(t * hpt - 1, 0), 0)),
                  const((1, d)), const((d, 2 * d_ff)), const((FFN_KERNEL, 2 * d_ff)),
                  const((1, 2 * d_ff)), const((d_ff, d)), const((1, d))],
        out_specs=pl.BlockSpec((None, tm, d), lambda i, t: (i, t, 0)),
        out_shape=jax.ShapeDtypeStruct((b, s, d), F32),
        scratch_shapes=[pltpu.VMEM((tm, d_ff), BF16)],
        compiler_params=_cparams("parallel", "parallel"),
        name="conv_ffn",
    )(h, h, g_pre.reshape(1, d), w_up, conv_w, conv_b.reshape(1, 2 * d_ff), w_down,
      g_post.reshape(1, d))


def kernel(x, mem, positions, mix_norm_pre, w_in, cv_w, cv_b, cv_ln_g, cv_ln_b, cv_pw_w, cv_pw_b,
           w_out, mix_norm_post, x_norm_pre, mem_norm, x_wq, x_wk, x_wv, x_wo, x_norm_post,
           ffn_norm_pre, ffn_w_up, ffn_conv_w, ffn_conv_b, ffn_w_down, ffn_norm_post):
    b, s, d = x.shape
    n_mem = mem.shape[1]
    depth = w_in.shape[0]
    m = b * s
    tm = 512

    cos_t, sin_t = (tab.reshape(m, LANES) for tab in rope_tables(positions))
    mem2 = mem.reshape(b * n_mem, d)
    h = x
    for l in range(depth):
        u = mixer_in_proj(h.reshape(m, d), mix_norm_pre[l], w_in[l].astype(BF16), cos_t, sin_t,
                          tm=tm).reshape(b, s, -1)
        a_out = stick_breaking(u)
        b_out = conformer_conv(u, cv_w[l], cv_b[l], cv_ln_g[l], cv_ln_b[l],
                               cv_pw_w[l].astype(BF16), cv_pw_b[l])
        c_out = dilated_attention(u)
        wkv = jnp.concatenate([x_wk[l], x_wv[l]], axis=1).astype(BF16)
        kv = norm_matmul(mem2, mem_norm[l], wkv, out_dtype=BF16, tm=tm).reshape(b, n_mem, 2 * d)
        h = post_mixer(a_out, b_out, c_out, h, w_out[l].astype(BF16), mix_norm_post[l], x_norm_pre[l],
                       x_wq[l].astype(BF16), kv, x_wo[l].astype(BF16), x_norm_post[l])
        h = conv_ffn(h, ffn_norm_pre[l], ffn_w_up[l].astype(BF16), ffn_conv_w[l],
                     ffn_conv_b[l], ffn_w_down[l].astype(BF16), ffn_norm_post[l])
    return h
```

```python
import functools

import jax
import jax.numpy as jnp
from jax import lax
from jax.experimental import pallas as pl
from jax.experimental.pallas import tpu as pltpu

F32 = jnp.float32
BF16 = jnp.bfloat16

EPS = 1e-6
HEAD_DIM = 64
LANES = 128
SB_HEADS = 4
CV_WIDTH = 256
CV_KERNEL = 31
DL_HEADS = 8
DL_STEPS = 128
ROPE_THETA = 10000.0
X_HEADS = 4
FFN_KERNEL = 3

VMEM_LIMIT = 56 * 1024 * 1024

NT_DIMS = (((1,), (1,)), ((), ()))


def _cparams(*sem):
    return pltpu.CompilerParams(dimension_semantics=sem, vmem_limit_bytes=VMEM_LIMIT)


def _rms(x, g):
    ms = jnp.mean(x * x, axis=-1, keepdims=True)
    return x * lax.rsqrt(ms + EPS) * g


def _norm_matmul_kernel(x_ref, g_ref, w_ref, o_ref, *, n_chunk):
    xn = _rms(x_ref[...], g_ref[...]).astype(BF16)
    n = w_ref.shape[1]
    for c in range(0, n, n_chunk):
        y = jnp.dot(xn, w_ref[:, c:c + n_chunk], preferred_element_type=F32)
        o_ref[:, c:c + n_chunk] = y.astype(o_ref.dtype)


def norm_matmul(x, g, w, *, out_dtype, tm, n_chunk=256):
    m, k = x.shape
    n = w.shape[1]
    return pl.pallas_call(
        functools.partial(_norm_matmul_kernel, n_chunk=n_chunk),
        grid=(m // tm,),
        in_specs=[pl.BlockSpec((tm, k), lambda i: (i, 0)),
                  pl.BlockSpec((1, k), lambda i: (0, 0)),
                  pl.BlockSpec((k, n), lambda i: (0, 0))],
        out_specs=pl.BlockSpec((tm, n), lambda i: (i, 0)),
        out_shape=jax.ShapeDtypeStruct((m, n), out_dtype),
        compiler_params=_cparams("parallel"),
        name="norm_matmul",
    )(x, g.reshape(1, k), w)


DL_FIRST_COL = 3 * SB_HEADS * HEAD_DIM + 2 * CV_WIDTH
DL_WIDTH = DL_HEADS * HEAD_DIM


def _rope(x, cos, sin):
    ln = lax.broadcasted_iota(jnp.int32, x.shape, 1)
    first_half = (ln % HEAD_DIM) < (HEAD_DIM // 2)
    partner = jnp.where(first_half, pltpu.roll(x, LANES - HEAD_DIM // 2, axis=1),
                        pltpu.roll(x, HEAD_DIM // 2, axis=1))
    return x * cos + partner * sin


def _in_proj_kernel(x_ref, g_ref, w_ref, cos_ref, sin_ref, o_ref, *, n_chunk):
    xn = _rms(x_ref[...], g_ref[...]).astype(BF16)
    cos = cos_ref[...]
    sin = sin_ref[...]
    k_col = DL_FIRST_COL + DL_WIDTH
    v_col = k_col + DL_WIDTH
    for c in range(0, w_ref.shape[1], n_chunk):
        y = jnp.dot(xn, w_ref[:, c:c + n_chunk], preferred_element_type=F32)
        for col in range(c, c + n_chunk, LANES):
            blk = y[:, col - c:col - c + LANES]
            if DL_FIRST_COL <= col < k_col:
                blk = _rope(blk, cos, sin) * HEAD_DIM ** -0.5
            elif k_col <= col < v_col:
                blk = _rope(blk, cos, sin)
            o_ref[:, col:col + LANES] = blk


def mixer_in_proj(x, g, w, cos_t, sin_t, *, tm, n_chunk=256):
    m, k = x.shape
    n = w.shape[1]
    return pl.pallas_call(
        functools.partial(_in_proj_kernel, n_chunk=n_chunk),
        grid=(m // tm,),
        in_specs=[pl.BlockSpec((tm, k), lambda i: (i, 0)),
                  pl.BlockSpec((1, k), lambda i: (0, 0)),
                  pl.BlockSpec((k, n), lambda i: (0, 0)),
                  pl.BlockSpec((tm, LANES), lambda i: (i, 0)),
                  pl.BlockSpec((tm, LANES), lambda i: (i, 0))],
        out_specs=pl.BlockSpec((tm, n), lambda i: (i, 0)),
        out_shape=jax.ShapeDtypeStruct((m, n), F32),
        compiler_params=_cparams("parallel"),
        name="mixer_in_proj",
    )(x, g.reshape(1, k), w, cos_t, sin_t)


SB_MASKED_Z = -1e4
SB_DEAD = -104.0


def _sb_kernel(q_ref, k_ref, v_ref, o_ref, kt_ref, vb_ref, *, seq, tq, n_fast, q_group):
    scale = HEAD_DIM ** -0.5
    vb_ref[...] = v_ref[...].astype(BF16)

    def transpose_keys(c, _):
        c0 = pl.multiple_of(c * LANES, LANES)
        kt_ref[:, pl.ds(c0, LANES)] = k_ref[pl.ds(c0, LANES), :].T.astype(BF16)
        return 0

    lax.fori_loop(0, seq // LANES, transpose_keys, 0)

    head0 = lax.broadcasted_iota(jnp.int32, (tq, LANES), 1) < HEAD_DIM
    row = lax.broadcasted_iota(jnp.int32, (2 * tq, tq), 0) % tq
    col = lax.broadcasted_iota(jnp.int32, (2 * tq, tq), 1)
    before = col < row
    krow = lax.broadcasted_iota(jnp.int32, (tq, tq), 0)
    kcol = lax.broadcasted_iota(jnp.int32, (tq, tq), 1)
    later = jnp.where(krow > kcol, 1.0, 0.0).astype(BF16)
    later2 = jnp.concatenate([later, later], axis=0)

    def tile(q2, kt_blk, vblk, carry, ok):
        z = jnp.dot(q2, kt_blk, preferred_element_type=F32)
        if ok is not None:
            z = jnp.where(ok, z, SB_MASKED_Z)
        softplus = jnp.log(1.0 + jnp.exp(-jnp.abs(z)))
        log_beta = jnp.minimum(z, 0.0) - softplus
        log_keep = log_beta - z
        hi = log_keep.astype(BF16)
        lo = (log_keep - hi.astype(F32)).astype(BF16)
        between = jnp.dot(jnp.concatenate([hi, lo], axis=1), later2,
                          preferred_element_type=F32) + carry
        a = jnp.exp(log_beta + between)
        pv = jnp.dot(a.astype(BF16), vblk, preferred_element_type=F32)
        return pv, carry + jnp.sum(log_keep, axis=1, keepdims=True)

    def near_blocks(qb):
        q0 = pl.multiple_of(qb * tq, tq)
        q = q_ref[pl.ds(q0, tq), :] * scale
        q2 = jnp.concatenate([jnp.where(head0, q, 0.0), jnp.where(head0, 0.0, q)], axis=0).astype(BF16)
        acc = jnp.zeros((2 * tq, LANES), F32)
        carry = jnp.zeros((2 * tq, 1), F32)
        for j in range(n_fast):
            kblock = qb - j
            k0 = pl.multiple_of(jnp.maximum(kblock, 0) * tq, tq)
            ok = before if j == 0 else kblock >= 0
            pv, carry = tile(q2, kt_ref[:, pl.ds(k0, tq)], vb_ref[pl.ds(k0, tq), :], carry, ok)
            acc = acc + pv
        return q2, acc, carry

    def far_blocks(qb, q2, acc, carry):
        def alive(st):
            kblock, _, c = st
            return (kblock >= 0) & (jnp.max(c) > SB_DEAD)

        def k_body(st):
            kblock, a, c = st
            k0 = pl.multiple_of(kblock * tq, tq)
            pv, c = tile(q2, kt_ref[:, pl.ds(k0, tq)], vb_ref[pl.ds(k0, tq), :], c, None)
            return kblock - 1, a + pv, c

        return lax.while_loop(alive, k_body, (qb - n_fast, acc, carry))[1]

    def q_body(i, _):
        started = [(i * q_group + g,) + near_blocks(i * q_group + g) for g in range(q_group)]
        for qb, q2, acc, carry in started:
            acc = far_blocks(qb, q2, acc, carry)
            q0 = pl.multiple_of(qb * tq, tq)
            o_ref[pl.ds(q0, tq), :] = jnp.where(head0, acc[:tq], acc[tq:]).astype(o_ref.dtype)
        return 0

    lax.fori_loop(0, seq // tq // q_group, q_body, 0)


def stick_breaking(u, *, tq=256, n_fast=2, q_group=4):
    b, s, _ = u.shape
    pairs = SB_HEADS // 2
    blk = lambda off: pl.BlockSpec((None, s, LANES), lambda i, p: (i, 0, off + p))
    return pl.pallas_call(
        functools.partial(_sb_kernel, seq=s, tq=tq, n_fast=n_fast, q_group=q_group),
        grid=(b, pairs),
        in_specs=[blk(0), blk(pairs), blk(2 * pairs)],
        out_specs=pl.BlockSpec((None, s, LANES), lambda i, p: (i, 0, p)),
        out_shape=jax.ShapeDtypeStruct((b, s, SB_HEADS * HEAD_DIM), BF16),
        scratch_shapes=[pltpu.VMEM((LANES, s), BF16), pltpu.VMEM((s, LANES), BF16)],
        compiler_params=_cparams("parallel", "parallel"),
        name="stick_breaking",
    )(u, u, u)


CV_HALO = 32


def _conv_kernel(val_ref, gate_ref, hval_ref, hgate_ref, w_ref, b_ref, lng_ref, lnb_ref,
                 pw_ref, pwb_ref, o_ref, g_scr, *, tm):
    first = pl.program_id(1) == 0
    g_scr[CV_HALO:, :] = val_ref[...] * jax.nn.sigmoid(gate_ref[...])
    halo = hval_ref[...] * jax.nn.sigmoid(hgate_ref[...])
    g_scr[:CV_HALO, :] = jnp.where(first, 0.0, halo)
    w = w_ref[...]
    g_ext = g_scr[...]
    acc = jnp.zeros((tm, CV_WIDTH), F32) + b_ref[...]
    rolled = {}
    for k in range(CV_KERNEL):
        lo = CV_HALO - (CV_KERNEL - 1) + k
        start = -(-lo // 8) * 8
        shift = start - lo
        if shift not in rolled:
            rolled[shift] = pltpu.roll(g_ext, shift, axis=0) if shift else g_ext
        acc = acc + rolled[shift][start:start + tm, :] * w[k:k + 1, :]
    mu = jnp.mean(acc, axis=-1, keepdims=True)
    xc = acc - mu
    var = jnp.mean(xc * xc, axis=-1, keepdims=True)
    y = xc * lax.rsqrt(var + EPS) * lng_ref[...] + lnb_ref[...]
    y = y * jax.nn.sigmoid(y)
    out = jnp.dot(y.astype(BF16), pw_ref[...], preferred_element_type=F32) + pwb_ref[...]
    o_ref[...] = out.astype(o_ref.dtype)


def conformer_conv(u, cv_w, cv_b, ln_g, ln_b, pw_w, pw_b, *, tm=512):
    b, s, _ = u.shape
    c = CV_WIDTH
    val_blk = 3 * SB_HEADS * HEAD_DIM // c
    hpt = tm // CV_HALO
    cur = lambda off: pl.BlockSpec((None, tm, c), lambda i, t: (i, t, off))
    halo = lambda off: pl.BlockSpec((None, CV_HALO, c),
                                    lambda i, t: (i, jnp.maximum(t * hpt - 1, 0), off))
    vec = pl.BlockSpec((1, c), lambda i, t: (0, 0))
    return pl.pallas_call(
        functools.partial(_conv_kernel, tm=tm),
        grid=(b, s // tm),
        in_specs=[cur(val_blk), cur(val_blk + 1), halo(val_blk), halo(val_blk + 1),
                  pl.BlockSpec((CV_KERNEL, c), lambda i, t: (0, 0)), vec, vec, vec,
                  pl.BlockSpec((c, c), lambda i, t: (0, 0)), vec],
        out_specs=pl.BlockSpec((None, tm, c), lambda i, t: (i, t, 0)),
        out_shape=jax.ShapeDtypeStruct((b, s, c), BF16),
        scratch_shapes=[pltpu.VMEM((tm + CV_HALO, c), F32)],
        compiler_params=_cparams("parallel", "parallel"),
        name="conformer_conv",
    )(u, u, u, u, cv_w, cv_b.reshape(1, c), ln_g.reshape(1, c), ln_b.reshape(1, c),
      pw_w, pw_b.reshape(1, c))


def _rope_table_kernel(pos_ref, invf_ref, cos_ref, sin_ref, *, seq):
    half = HEAD_DIM // 2
    ang = invf_ref[...] * pos_ref[...].astype(F32)
    c = jnp.cos(ang)
    s = jnp.sin(ang)
    cos_t = jnp.concatenate([c, c, c, c], axis=0)
    sin_t = jnp.concatenate([-s, s, -s, s], axis=0)
    del half
    for j in range(seq // LANES):
        cos_ref[j * LANES:(j + 1) * LANES, :] = cos_t[:, j * LANES:(j + 1) * LANES].T
        sin_ref[j * LANES:(j + 1) * LANES, :] = sin_t[:, j * LANES:(j + 1) * LANES].T


def rope_tables(positions):
    b, s = positions.shape
    half = HEAD_DIM // 2
    inv_freq = ROPE_THETA ** (-jnp.arange(half, dtype=F32) / half)
    out = jax.ShapeDtypeStruct((b, s, LANES), F32)
    return pl.pallas_call(
        functools.partial(_rope_table_kernel, seq=s),
        grid=(b,),
        in_specs=[pl.BlockSpec((None, 1, s), lambda i: (i, 0, 0)),
                  pl.BlockSpec((half, 1), lambda i: (0, 0))],
        out_specs=[pl.BlockSpec((None, s, LANES), lambda i: (i, 0, 0))] * 2,
        out_shape=[out, out],
        compiler_params=_cparams("parallel"),
        name="rope_tables",
    )(positions.reshape(b, 1, s), inv_freq.reshape(half, 1))


DL_DILATIONS = (1, 4, 16)
DL_GROUP = 8
DL_PAD = DL_STEPS


def _dil_kernel(q_ref, k_ref, v_ref, o_ref, q0s, q1s, kts, vs, stq, stk, stv, accd, md, ld, acct, mt, lt,
                *, seq):
    t = DL_STEPS
    sub = seq // 4
    head0 = lax.broadcasted_iota(jnp.int32, (t, LANES), 1) < HEAD_DIM
    qi = lax.broadcasted_iota(jnp.int32, (2 * t, 2 * t), 0) % t
    kj = lax.broadcasted_iota(jnp.int32, (2 * t, 2 * t), 1)
    band = (kj >= qi) & (kj <= qi + t)
    band_first = band & (kj >= t)

    def valid_mask(n):
        return band & (kj >= jnp.where(n == 0, t, 0))

    def emit(li, dst0, q, k, v):
        q0s[li, pl.ds(dst0, t), :] = jnp.where(head0, q, 0.0).astype(BF16)
        q1s[li, pl.ds(dst0, t), :] = jnp.where(head0, 0.0, q).astype(BF16)
        kts[li, :, pl.ds(DL_PAD + dst0, t)] = k.T.astype(BF16)
        vs[li, pl.ds(DL_PAD + dst0, t), :] = v.astype(BF16)

    for li in range(len(DL_DILATIONS)):
        kts[li, :, :DL_PAD] = jnp.zeros((LANES, DL_PAD), BF16)
        vs[li, :DL_PAD, :] = jnp.zeros((DL_PAD, LANES), BF16)

    def nat_body(c, _):
        rows = pl.ds(pl.multiple_of(c * t, t), t)
        emit(0, pl.multiple_of(c * t, t), q_ref[rows, :], k_ref[rows, :], v_ref[rows, :])
        return 0

    lax.fori_loop(0, seq // t, nat_body, 0, unroll=4)

    blocks = sub // (4 * t)

    def d4_d16_body(c, _):
        res4 = c // blocks
        m0 = (c % blocks) * 4 * t
        src = pl.ds(res4 + 4 * m0, 4 * t, stride=4)
        stq[...] = q_ref[src, :]
        stk[...] = k_ref[src, :]
        stv[...] = v_ref[src, :]
        for part in range(4):
            rows = pl.ds(part * t, t)
            emit(1, pl.multiple_of(res4 * sub + m0 + part * t, t), stq[rows, :], stk[rows, :], stv[rows, :])
        for s4 in range(4):
            rows = pl.ds(s4, t, stride=4)
            dst0 = pl.multiple_of((res4 + 4 * s4) * (seq // 16) + m0 // 4, t)
            emit(2, dst0, stq[rows, :], stk[rows, :], stv[rows, :])
        return 0

    lax.fori_loop(0, 4 * blocks, d4_d16_body, 0)

    def tile(li, r0, valid, prev):
        q2 = jnp.concatenate([q0s[li, pl.ds(r0, t), :], q1s[li, pl.ds(r0, t), :]], axis=0)
        kt = kts[li, :, pl.ds(r0, 2 * t)]
        vw = vs[li, pl.ds(r0, 2 * t), :]
        s = jnp.where(valid, jnp.dot(q2, kt, preferred_element_type=F32), -jnp.inf)
        m2 = jnp.max(s, axis=1, keepdims=True)
        p = jnp.exp(s - m2)
        l2 = jnp.sum(p, axis=1, keepdims=True)
        pv = jnp.dot(p.astype(BF16), vw, preferred_element_type=F32)
        m_new = jnp.where(head0, m2[:t], m2[t:])
        l_new = jnp.where(head0, l2[:t], l2[t:])
        acc_new = jnp.where(head0, pv[:t], pv[t:])
        if prev is not None:
            m_prev, l_prev, acc_prev = prev
            m_tile = m_new
            m_new = jnp.maximum(m_prev, m_tile)
            a_prev = jnp.exp(m_prev - m_new)
            a_tile = jnp.exp(m_tile - m_new)
            l_new = a_prev * l_prev + a_tile * l_new
            acc_new = a_prev * acc_prev + a_tile * acc_new
        return m_new, l_new, acc_new

    grp = DL_GROUP
    n_trips = seq // t // grp

    def d4_group(i, _):
        n0 = (i * grp) % (sub // t)
        outs = []
        for g in range(grp):
            r0 = pl.multiple_of((i * grp + g) * t, t)
            outs.append((r0, tile(1, r0, valid_mask(n0) if g == 0 else band, None)))
        for r0, (m_new, l_new, acc_new) in outs:
            md[pl.ds(r0, t), :] = m_new
            ld[pl.ds(r0, t), :] = l_new
            accd[pl.ds(r0, t), :] = acc_new
        return 0

    lax.fori_loop(0, n_trips, d4_group, 0)

    nb16 = seq // 16 // t

    def d16_group(i, _):
        outs = []
        for g in range(grp):
            idx = i * grp + g
            res16 = idx // nb16
            n = g % nb16
            rows = pl.ds((res16 % 4) * sub + res16 // 4 + 4 * t * n, t, stride=4)
            prev = (md[rows, :], ld[rows, :], accd[rows, :])
            r0 = pl.multiple_of(idx * t, t)
            outs.append((rows, tile(2, r0, band_first if n == 0 else band, prev)))
        for rows, (m_new, l_new, acc_new) in outs:
            md[rows, :] = m_new
            ld[rows, :] = l_new
            accd[rows, :] = acc_new
        return 0

    lax.fori_loop(0, n_trips, d16_group, 0)

    def d1_group(i, _):
        outs = []
        for g in range(grp):
            n = i * grp + g
            r0 = pl.multiple_of(n * t, t)
            for r4 in range(4):
                src = pl.ds(r4 * sub + n * (t // 4), t // 4)
                dst = pl.ds(g * t + r4, t // 4, stride=4)
                mt[dst, :] = md[src, :]
                lt[dst, :] = ld[src, :]
                acct[dst, :] = accd[src, :]
            tmp = pl.ds(g * t, t)
            _, l_new, acc_new = tile(0, r0, valid_mask(n) if g == 0 else band,
                                     (mt[tmp, :], lt[tmp, :], acct[tmp, :]))
            outs.append((r0, acc_new / l_new))
        for r0, out in outs:
            o_ref[pl.ds(r0, t), :] = out.astype(o_ref.dtype)
        return 0

    lax.fori_loop(0, n_trips, d1_group, 0)


def dilated_attention(u):
    b, s, _ = u.shape
    assert s % (16 * DL_STEPS) == 0 and (s // 4 // DL_STEPS) % DL_GROUP == 0
    assert DL_GROUP % (s // 16 // DL_STEPS) == 0
    pairs = DL_HEADS // 2
    first = DL_FIRST_COL // LANES
    blk = lambda off: pl.BlockSpec((None, s, LANES), lambda i, p: (i, 0, off + p))
    nl = len(DL_DILATIONS)
    stat = pltpu.VMEM((s, LANES), F32)
    small = pltpu.VMEM((DL_GROUP * DL_STEPS, LANES), F32)
    stage = pltpu.VMEM((4 * DL_STEPS, LANES), F32)
    return pl.pallas_call(
        functools.partial(_dil_kernel, seq=s),
        grid=(b, pairs),
        in_specs=[blk(first), blk(first + pairs), blk(first + 2 * pairs)],
        out_specs=pl.BlockSpec((None, s, LANES), lambda i, p: (i, 0, p)),
        out_shape=jax.ShapeDtypeStruct((b, s, DL_HEADS * HEAD_DIM), BF16),
        scratch_shapes=[pltpu.VMEM((nl, s, LANES), BF16), pltpu.VMEM((nl, s, LANES), BF16),
                        pltpu.VMEM((nl, LANES, DL_PAD + s), BF16), pltpu.VMEM((nl, DL_PAD + s, LANES), BF16),
                        stage, stage, stage, stat, stat, stat, small, small, small],
        compiler_params=_cparams("parallel", "parallel"),
        name="dilated_attention",
    )(u, u, u)


def _post_mixer_kernel(a_ref, b_ref, c_ref, h_ref, wa_ref, wb_ref, wc_ref, gmix_ref, gpre_ref, wq_ref,
                       kv_ref, wo_ref, gpost_ref, o_ref, att_ref, *, d_model):
    y = jnp.dot(a_ref[...], wa_ref[...], preferred_element_type=F32)
    y = y + jnp.dot(b_ref[...], wb_ref[...], preferred_element_type=F32)
    y = y + jnp.dot(c_ref[...], wc_ref[...], preferred_element_type=F32)
    h = h_ref[...] + _rms(y, gmix_ref[...])

    hd = d_model // X_HEADS
    qn = _rms(h, gpre_ref[...]).astype(BF16)
    q = (jnp.dot(qn, wq_ref[...], preferred_element_type=F32) * hd ** -0.5).astype(BF16)
    for hh in range(X_HEADS):
        k = kv_ref[:, hh * hd:(hh + 1) * hd]
        v = kv_ref[:, d_model + hh * hd:d_model + (hh + 1) * hd]
        s = lax.dot_general(q[:, hh * hd:(hh + 1) * hd], k, NT_DIMS, preferred_element_type=F32)
        p = jnp.exp(s - jnp.max(s, axis=1, keepdims=True))
        den = jnp.sum(p, axis=1, keepdims=True)
        att = jnp.dot(p.astype(BF16), v, preferred_element_type=F32) / den
        att_ref[:, hh * hd:(hh + 1) * hd] = att.astype(BF16)
    y = jnp.dot(att_ref[...], wo_ref[...], preferred_element_type=F32)
    o_ref[...] = h + _rms(y, gpost_ref[...])


def post_mixer(a_out, b_out, c_out, h, w_out, g_mix_post, g_x_pre, wq, kv, wo, g_x_post, *, tm=512):
    b, s, d = h.shape
    n_mem = kv.shape[1]
    wa_rows = a_out.shape[2]
    wb_rows = b_out.shape[2]
    rows = lambda w: pl.BlockSpec((None, tm, w), lambda i, t: (i, t, 0))
    const = lambda shape: pl.BlockSpec(shape, lambda i, t: (0, 0))
    vec = const((1, d))
    return pl.pallas_call(
        functools.partial(_post_mixer_kernel, d_model=d),
        grid=(b, s // tm),
        in_specs=[rows(wa_rows), rows(wb_rows), rows(c_out.shape[2]), rows(d),
                  const((wa_rows, d)), const((wb_rows, d)), const((c_out.shape[2], d)), vec, vec,
                  const((d, d)), pl.BlockSpec((None, n_mem, 2 * d), lambda i, t: (i, 0, 0)),
                  const((d, d)), vec],
        out_specs=rows(d),
        out_shape=jax.ShapeDtypeStruct((b, s, d), F32),
        scratch_shapes=[pltpu.VMEM((tm, d), BF16)],
        compiler_params=_cparams("parallel", "parallel"),
        name="post_mixer",
    )(a_out, b_out, c_out, h, w_out[:wa_rows], w_out[wa_rows:wa_rows + wb_rows],
      w_out[wa_rows + wb_rows:], g_mix_post.reshape(1, d), g_x_pre.reshape(1, d), wq, kv, wo,
      g_x_post.reshape(1, d))


FFN_HALO = 8


def _ffn_kernel(h_ref, halo_ref, gpre_ref, wup_ref, cw_ref, cb_ref, wdn_ref, gpost_ref, o_ref, act_ref,
                *, tm, d_ff, n_chunk):
    first = pl.program_id(1) == 0
    h = h_ref[...]
    g = gpre_ref[...]
    halo = jnp.where(first, 0.0, _rms(halo_ref[...], g))
    xn = jnp.concatenate([halo, _rms(h, g)], axis=0).astype(BF16)

    def conv(u, c0):
        y = cb_ref[:, c0:c0 + n_chunk] + u[FFN_HALO:, :] * cw_ref[FFN_KERNEL - 1:FFN_KERNEL, c0:c0 + n_chunk]
        for k in range(FFN_KERNEL - 1):
            past = pltpu.roll(u, FFN_KERNEL - 1 - k, axis=0)[FFN_HALO:, :]
            y = y + past * cw_ref[k:k + 1, c0:c0 + n_chunk]
        return y

    for c0 in range(0, d_ff, n_chunk):
        ug = jnp.dot(xn, wup_ref[:, c0:c0 + n_chunk], preferred_element_type=F32)
        uv = jnp.dot(xn, wup_ref[:, d_ff + c0:d_ff + c0 + n_chunk], preferred_element_type=F32)
        act = jax.nn.gelu(conv(ug, c0), approximate=True) * conv(uv, d_ff + c0)
        act_ref[:, c0:c0 + n_chunk] = act.astype(BF16)
    y = jnp.dot(act_ref[...], wdn_ref[...], preferred_element_type=F32)
    o_ref[...] = h + _rms(y, gpost_ref[...])


def conv_ffn(h, g_pre, w_up, conv_w, conv_b, w_down, g_post, *, tm=512, n_chunk=256):
    b, s, d = h.shape
    d_ff = w_down.shape[0]
    hpt = tm // FFN_HALO
    const = lambda shape: pl.BlockSpec(shape, lambda i, t: (0, 0))
    return pl.pallas_call(
        functools.partial(_ffn_kernel, tm=tm, d_ff=d_ff, n_chunk=n_chunk),
        grid=(b, s // tm),
        in_specs=[pl.BlockSpec((None, tm, d), lambda i, t: (i, t, 0)),
                  pl.BlockSpec((None, FFN_HALO, d), lambda i, t: (i, jnp.maximum(t * hpt - 1, 0), 0)),
                  const((1, d)), const((d, 2 * d_ff)), const((FFN_KERNEL, 2 * d_ff)),
                  const((1, 2 * d_ff)), const((d_ff, d)), const((1, d))],
        out_specs=pl.BlockSpec((None, tm, d), lambda i, t: (i, t, 0)),
        out_shape=jax.ShapeDtypeStruct((b, s, d), F32),
        scratch_shapes=[pltpu.VMEM((tm, d_ff), BF16)],
        compiler_params=_cparams("parallel", "parallel"),
        name="conv_ffn",
    )(h, h, g_pre.reshape(1, d), w_up, conv_w, conv_b.reshape(1, 2 * d_ff), w_down,
      g_post.reshape(1, d))


def kernel(x, mem, positions, mix_norm_pre, w_in, cv_w, cv_b, cv_ln_g, cv_ln_b, cv_pw_w, cv_pw_b,
           w_out, mix_norm_post, x_norm_pre, mem_norm, x_wq, x_wk, x_wv, x_wo, x_norm_post,
           ffn_norm_pre, ffn_w_up, ffn_conv_w, ffn_conv_b, ffn_w_down, ffn_norm_post):
    b, s, d = x.shape
    n_mem = mem.shape[1]
    depth = w_in.shape[0]
    m = b * s
    tm = 512

    cos_t, sin_t = (tab.reshape(m, LANES) for tab in rope_tables(positions))
    mem2 = mem.reshape(b * n_mem, d)
    h = x
    for l in range(depth):
        u = mixer_in_proj(h.reshape(m, d), mix_norm_pre[l], w_in[l].astype(BF16), cos_t, sin_t,
                          tm=tm).reshape(b, s, -1)
        a_out = stick_breaking(u)
        b_out = conformer_conv(u, cv_w[l], cv_b[l], cv_ln_g[l], cv_ln_b[l],
                               cv_pw_w[l].astype(BF16), cv_pw_b[l])
        c_out = dilated_attention(u)
        wkv = jnp.concatenate([x_wk[l], x_wv[l]], axis=1).astype(BF16)
        kv = norm_matmul(mem2, mem_norm[l], wkv, out_dtype=BF16, tm=tm).reshape(b, n_mem, 2 * d)
        h = post_mixer(a_out, b_out, c_out, h, w_out[l].astype(BF16), mix_norm_post[l], x_norm_pre[l],
                       x_wq[l].astype(BF16), kv, x_wo[l].astype(BF16), x_norm_post[l])
        h = conv_ffn(h, ffn_norm_pre[l], ffn_w_up[l].astype(BF16), ffn_conv_w[l],
                     ffn_conv_b[l], ffn_w_down[l].astype(BF16), ffn_norm_post[l])
    return h
```

```python
import functools

import jax
import jax.numpy as jnp
from jax import lax
from jax.experimental import pallas as pl
from jax.experimental.pallas import tpu as pltpu

F32 = jnp.float32
BF16 = jnp.bfloat16

EPS = 1e-6
HEAD_DIM = 64
LANES = 128
SB_HEADS = 4
CV_WIDTH = 256
CV_KERNEL = 31
DL_HEADS = 8
DL_STEPS = 128
ROPE_THETA = 10000.0
X_HEADS = 4
FFN_KERNEL = 3

VMEM_LIMIT = 56 * 1024 * 1024

NT_DIMS = (((1,), (1,)), ((), ()))


def _cparams(*sem):
    return pltpu.CompilerParams(dimension_semantics=sem, vmem_limit_bytes=VMEM_LIMIT)


def _rms(x, g):
    ms = jnp.mean(x * x, axis=-1, keepdims=True)
    return x * lax.rsqrt(ms + EPS) * g


def _norm_matmul_kernel(x_ref, g_ref, w_ref, o_ref, *, n_chunk):
    xn = _rms(x_ref[...], g_ref[...]).astype(BF16)
    n = w_ref.shape[1]
    for c in range(0, n, n_chunk):
        y = jnp.dot(xn, w_ref[:, c:c + n_chunk], preferred_element_type=F32)
        o_ref[:, c:c + n_chunk] = y.astype(o_ref.dtype)


def norm_matmul(x, g, w, *, out_dtype, tm, n_chunk=256):
    m, k = x.shape
    n = w.shape[1]
    return pl.pallas_call(
        functools.partial(_norm_matmul_kernel, n_chunk=n_chunk),
        grid=(m // tm,),
        in_specs=[pl.BlockSpec((tm, k), lambda i: (i, 0)),
                  pl.BlockSpec((1, k), lambda i: (0, 0)),
                  pl.BlockSpec((k, n), lambda i: (0, 0))],
        out_specs=pl.BlockSpec((tm, n), lambda i: (i, 0)),
        out_shape=jax.ShapeDtypeStruct((m, n), out_dtype),
        compiler_params=_cparams("parallel"),
        name="norm_matmul",
    )(x, g.reshape(1, k), w)


DL_FIRST_COL = 3 * SB_HEADS * HEAD_DIM + 2 * CV_WIDTH
DL_WIDTH = DL_HEADS * HEAD_DIM


def _rope(x, cos, sin):
    ln = lax.broadcasted_iota(jnp.int32, x.shape, 1)
    first_half = (ln % HEAD_DIM) < (HEAD_DIM // 2)
    partner = jnp.where(first_half, pltpu.roll(x, LANES - HEAD_DIM // 2, axis=1),
                        pltpu.roll(x, HEAD_DIM // 2, axis=1))
    return x * cos + partner * sin


def _in_proj_kernel(x_ref, g_ref, w_ref, cos_ref, sin_ref, o_ref, *, n_chunk):
    xn = _rms(x_ref[...], g_ref[...]).astype(BF16)
    cos = cos_ref[...]
    sin = sin_ref[...]
    k_col = DL_FIRST_COL + DL_WIDTH
    v_col = k_col + DL_WIDTH
    for c in range(0, w_ref.shape[1], n_chunk):
        y = jnp.dot(xn, w_ref[:, c:c + n_chunk], preferred_element_type=F32)
        for col in range(c, c + n_chunk, LANES):
            blk = y[:, col - c:col - c + LANES]
            if DL_FIRST_COL <= col < k_col:
                blk = _rope(blk, cos, sin) * HEAD_DIM ** -0.5
            elif k_col <= col < v_col:
                blk = _rope(blk, cos, sin)
            o_ref[:, col:col + LANES] = blk


def mixer_in_proj(x, g, w, cos_t, sin_t, *, tm, n_chunk=256):
    m, k = x.shape
    n = w.shape[1]
    return pl.pallas_call(
        functools.partial(_in_proj_kernel, n_chunk=n_chunk),
        grid=(m // tm,),
        in_specs=[pl.BlockSpec((tm, k), lambda i: (i, 0)),
                  pl.BlockSpec((1, k), lambda i: (0, 0)),
                  pl.BlockSpec((k, n), lambda i: (0, 0)),
                  pl.BlockSpec((tm, LANES), lambda i: (i, 0)),
                  pl.BlockSpec((tm, LANES), lambda i: (i, 0))],
        out_specs=pl.BlockSpec((tm, n), lambda i: (i, 0)),
        out_shape=jax.ShapeDtypeStruct((m, n), F32),
        compiler_params=_cparams("parallel"),
        name="mixer_in_proj",
    )(x, g.reshape(1, k), w, cos_t, sin_t)


SB_MASKED_Z = -1e4
SB_DEAD = -104.0


def _sb_kernel(q_ref, k_ref, v_ref, o_ref, kt_ref, vb_ref, *, seq, tq, n_fast, q_group):
    scale = HEAD_DIM ** -0.5
    vb_ref[...] = v_ref[...].astype(BF16)

    def transpose_keys(c, _):
        c0 = pl.multiple_of(c * LANES, LANES)
        kt_ref[:, pl.ds(c0, LANES)] = k_ref[pl.ds(c0, LANES), :].T.astype(BF16)
        return 0

    lax.fori_loop(0, seq // LANES, transpose_keys, 0)

    head0 = lax.broadcasted_iota(jnp.int32, (tq, LANES), 1) < HEAD_DIM
    row = lax.broadcasted_iota(jnp.int32, (2 * tq, tq), 0) % tq
    col = lax.broadcasted_iota(jnp.int32, (2 * tq, tq), 1)
    before = col < row
    krow = lax.broadcasted_iota(jnp.int32, (tq, tq), 0)
    kcol = lax.broadcasted_iota(jnp.int32, (tq, tq), 1)
    later = jnp.where(krow > kcol, 1.0, 0.0).astype(BF16)
    later2 = jnp.concatenate([later, later], axis=0)

    def tile(q2, kt_blk, vblk, carry, ok):
        z = jnp.dot(q2, kt_blk, preferred_element_type=F32)
        if ok is not None:
            z = jnp.where(ok, z, SB_MASKED_Z)
        softplus = jnp.log(1.0 + jnp.exp(-jnp.abs(z)))
        log_beta = jnp.minimum(z, 0.0) - softplus
        log_keep = log_beta - z
        hi = log_keep.astype(BF16)
        lo = (log_keep - hi.astype(F32)).astype(BF16)
        between = jnp.dot(jnp.concatenate([hi, lo], axis=1), later2,
                          preferred_element_type=F32) + carry
        a = jnp.exp(log_beta + between)
        pv = jnp.dot(a.astype(BF16), vblk, preferred_element_type=F32)
        return pv, carry + jnp.sum(log_keep, axis=1, keepdims=True)

    def near_blocks(qb):
        q0 = pl.multiple_of(qb * tq, tq)
        q = q_ref[pl.ds(q0, tq), :] * scale
        q2 = jnp.concatenate([jnp.where(head0, q, 0.0), jnp.where(head0, 0.0, q)], axis=0).astype(BF16)
        acc = jnp.zeros((2 * tq, LANES), F32)
        carry = jnp.zeros((2 * tq, 1), F32)
        for j in range(n_fast):
            kblock = qb - j
            k0 = pl.multiple_of(jnp.maximum(kblock, 0) * tq, tq)
            ok = before if j == 0 else kblock >= 0
            pv, carry = tile(q2, kt_ref[:, pl.ds(k0, tq)], vb_ref[pl.ds(k0, tq), :], carry, ok)
            acc = acc + pv
        return q2, acc, carry

    def far_blocks(qb, q2, acc, carry):
        def alive(st):
            kblock, _, c = st
            return (kblock >= 0) & (jnp.max(c) > SB_DEAD)

        def k_body(st):
            kblock, a, c = st
            k0 = pl.multiple_of(kblock * tq, tq)
            pv, c = tile(q2, kt_ref[:, pl.ds(k0, tq)], vb_ref[pl.ds(k0, tq), :], c, None)
            return kblock - 1, a + pv, c

        return lax.while_loop(alive, k_body, (qb - n_fast, acc, carry))[1]

    def q_body(i, _):
        started = [(i * q_group + g,) + near_blocks(i * q_group + g) for g in range(q_group)]
        for qb, q2, acc, carry in started:
            acc = far_blocks(qb, q2, acc, carry)
            q0 = pl.multiple_of(qb * tq, tq)
            o_ref[pl.ds(q0, tq), :] = jnp.where(head0, acc[:tq], acc[tq:]).astype(o_ref.dtype)
        return 0

    lax.fori_loop(0, seq // tq // q_group, q_body, 0)


def stick_breaking(u, *, tq=256, n_fast=2, q_group=4):
    b, s, _ = u.shape
    pairs = SB_HEADS // 2
    blk = lambda off: pl.BlockSpec((None, s, LANES), lambda i, p: (i, 0, off + p))
    return pl.pallas_call(
        functools.partial(_sb_kernel, seq=s, tq=tq, n_fast=n_fast, q_group=q_group),
        grid=(b, pairs),
        in_specs=[blk(0), blk(pairs), blk(2 * pairs)],
        out_specs=pl.BlockSpec((None, s, LANES), lambda i, p: (i, 0, p)),
        out_shape=jax.ShapeDtypeStruct((b, s, SB_HEADS * HEAD_DIM), BF16),
        scratch_shapes=[pltpu.VMEM((LANES, s), BF16), pltpu.VMEM((s, LANES), BF16)],
        compiler_params=_cparams("parallel", "parallel"),
        name="stick_breaking",
    )(u, u, u)


CV_HALO = 32


def _conv_kernel(val_ref, gate_ref, hval_ref, hgate_ref, w_ref, b_ref, lng_ref, lnb_ref,
                 pw_ref, pwb_ref, o_ref, g_scr, *, tm):
    first = pl.program_id(1) == 0
    g_scr[CV_HALO:, :] = val_ref[...] * jax.nn.sigmoid(gate_ref[...])
    halo = hval_ref[...] * jax.nn.sigmoid(hgate_ref[...])
    g_scr[:CV_HALO, :] = jnp.where(first, 0.0, halo)
    w = w_ref[...]
    g_ext = g_scr[...]
    acc = jnp.zeros((tm, CV_WIDTH), F32) + b_ref[...]
    rolled = {}
    for k in range(CV_KERNEL):
        lo = CV_HALO - (CV_KERNEL - 1) + k
        start = -(-lo // 8) * 8
        shift = start - lo
        if shift not in rolled:
            rolled[shift] = pltpu.roll(g_ext, shift, axis=0) if shift else g_ext
        acc = acc + rolled[shift][start:start + tm, :] * w[k:k + 1, :]
    mu = jnp.mean(acc, axis=-1, keepdims=True)
    xc = acc - mu
    var = jnp.mean(xc * xc, axis=-1, keepdims=True)
    y = xc * lax.rsqrt(var + EPS) * lng_ref[...] + lnb_ref[...]
    y = y * jax.nn.sigmoid(y)
    out = jnp.dot(y.astype(BF16), pw_ref[...], preferred_element_type=F32) + pwb_ref[...]
    o_ref[...] = out.astype(o_ref.dtype)


def conformer_conv(u, cv_w, cv_b, ln_g, ln_b, pw_w, pw_b, *, tm=1024):
    b, s, _ = u.shape
    c = CV_WIDTH
    val_blk = 3 * SB_HEADS * HEAD_DIM // c
    hpt = tm // CV_HALO
    cur = lambda off: pl.BlockSpec((None, tm, c), lambda i, t: (i, t, off))
    halo = lambda off: pl.BlockSpec((None, CV_HALO, c),
                                    lambda i, t: (i, jnp.maximum(t * hpt - 1, 0), off))
    vec = pl.BlockSpec((1, c), lambda i, t: (0, 0))
    return pl.pallas_call(
        functools.partial(_conv_kernel, tm=tm),
        grid=(b, s // tm),
        in_specs=[cur(val_blk), cur(val_blk + 1), halo(val_blk), halo(val_blk + 1),
                  pl.BlockSpec((CV_KERNEL, c), lambda i, t: (0, 0)), vec, vec, vec,
                  pl.BlockSpec((c, c), lambda i, t: (0, 0)), vec],
        out_specs=pl.BlockSpec((None, tm, c), lambda i, t: (i, t, 0)),
        out_shape=jax.ShapeDtypeStruct((b, s, c), BF16),
        scratch_shapes=[pltpu.VMEM((tm + CV_HALO, c), F32)],
        compiler_params=_cparams("parallel", "parallel"),
        name="conformer_conv",
    )(u, u, u, u, cv_w, cv_b.reshape(1, c), ln_g.reshape(1, c), ln_b.reshape(1, c),
      pw_w, pw_b.reshape(1, c))


def _rope_table_kernel(pos_ref, invf_ref, cos_ref, sin_ref, *, seq):
    half = HEAD_DIM // 2
    ang = invf_ref[...] * pos_ref[...].astype(F32)
    c = jnp.cos(ang)
    s = jnp.sin(ang)
    cos_t = jnp.concatenate([c, c, c, c], axis=0)
    sin_t = jnp.concatenate([-s, s, -s, s], axis=0)
    del half
    for j in range(seq // LANES):
        cos_ref[j * LANES:(j + 1) * LANES, :] = cos_t[:, j * LANES:(j + 1) * LANES].T
        sin_ref[j * LANES:(j + 1) * LANES, :] = sin_t[:, j * LANES:(j + 1) * LANES].T


def rope_tables(positions):
    b, s = positions.shape
    half = HEAD_DIM // 2
    inv_freq = ROPE_THETA ** (-jnp.arange(half, dtype=F32) / half)
    out = jax.ShapeDtypeStruct((b, s, LANES), F32)
    return pl.pallas_call(
        functools.partial(_rope_table_kernel, seq=s),
        grid=(b,),
        in_specs=[pl.BlockSpec((None, 1, s), lambda i: (i, 0, 0)),
                  pl.BlockSpec((half, 1), lambda i: (0, 0))],
        out_specs=[pl.BlockSpec((None, s, LANES), lambda i: (i, 0, 0))] * 2,
        out_shape=[out, out],
        compiler_params=_cparams("parallel"),
        name="rope_tables",
    )(positions.reshape(b, 1, s), inv_freq.reshape(half, 1))


DL_DILATIONS = (1, 4, 16)
DL_GROUP = 8
DL_PAD = DL_STEPS


def _dil_kernel(q_ref, k_ref, v_ref, o_ref, q0s, q1s, kts, vs, stq, stk, stv, accd, md, ld, acct, mt, lt,
                *, seq):
    t = DL_STEPS
    sub = seq // 4
    head0 = lax.broadcasted_iota(jnp.int32, (t, LANES), 1) < HEAD_DIM
    qi = lax.broadcasted_iota(jnp.int32, (2 * t, 2 * t), 0) % t
    kj = lax.broadcasted_iota(jnp.int32, (2 * t, 2 * t), 1)
    band = (kj >= qi) & (kj <= qi + t)
    band_first = band & (kj >= t)

    def valid_mask(n):
        return band & (kj >= jnp.where(n == 0, t, 0))

    def emit(li, dst0, q, k, v):
        q0s[li, pl.ds(dst0, t), :] = jnp.where(head0, q, 0.0).astype(BF16)
        q1s[li, pl.ds(dst0, t), :] = jnp.where(head0, 0.0, q).astype(BF16)
        kts[li, :, pl.ds(DL_PAD + dst0, t)] = k.T.astype(BF16)
        vs[li, pl.ds(DL_PAD + dst0, t), :] = v.astype(BF16)

    for li in range(len(DL_DILATIONS)):
        kts[li, :, :DL_PAD] = jnp.zeros((LANES, DL_PAD), BF16)
        vs[li, :DL_PAD, :] = jnp.zeros((DL_PAD, LANES), BF16)

    def nat_body(c, _):
        rows = pl.ds(pl.multiple_of(c * t, t), t)
        emit(0, pl.multiple_of(c * t, t), q_ref[rows, :], k_ref[rows, :], v_ref[rows, :])
        return 0

    lax.fori_loop(0, seq // t, nat_body, 0, unroll=4)

    blocks = sub // (4 * t)

    def d4_d16_body(c, _):
        res4 = c // blocks
        m0 = (c % blocks) * 4 * t
        src = pl.ds(res4 + 4 * m0, 4 * t, stride=4)
        stq[...] = q_ref[src, :]
        stk[...] = k_ref[src, :]
        stv[...] = v_ref[src, :]
        for part in range(4):
            rows = pl.ds(part * t, t)
            emit(1, pl.multiple_of(res4 * sub + m0 + part * t, t), stq[rows, :], stk[rows, :], stv[rows, :])
        for s4 in range(4):
            rows = pl.ds(s4, t, stride=4)
            dst0 = pl.multiple_of((res4 + 4 * s4) * (seq // 16) + m0 // 4, t)
            emit(2, dst0, stq[rows, :], stk[rows, :], stv[rows, :])
        return 0

    lax.fori_loop(0, 4 * blocks, d4_d16_body, 0)

    def tile(li, r0, valid, prev):
        q2 = jnp.concatenate([q0s[li, pl.ds(r0, t), :], q1s[li, pl.ds(r0, t), :]], axis=0)
        kt = kts[li, :, pl.ds(r0, 2 * t)]
        vw = vs[li, pl.ds(r0, 2 * t), :]
        s = jnp.where(valid, jnp.dot(q2, kt, preferred_element_type=F32), -jnp.inf)
        m2 = jnp.max(s, axis=1, keepdims=True)
        p = jnp.exp(s - m2)
        l2 = jnp.sum(p, axis=1, keepdims=True)
        pv = jnp.dot(p.astype(BF16), vw, preferred_element_type=F32)
        m_new = jnp.where(head0, m2[:t], m2[t:])
        l_new = jnp.where(head0, l2[:t], l2[t:])
        acc_new = jnp.where(head0, pv[:t], pv[t:])
        if prev is not None:
            m_prev, l_prev, acc_prev = prev
            m_tile = m_new
            m_new = jnp.maximum(m_prev, m_tile)
            a_prev = jnp.exp(m_prev - m_new)
            a_tile = jnp.exp(m_tile - m_new)
            l_new = a_prev * l_prev + a_tile * l_new
            acc_new = a_prev * acc_prev + a_tile * acc_new
        return m_new, l_new, acc_new

    grp = DL_GROUP
    n_trips = seq // t // grp

    def d4_group(i, _):
        n0 = (i * grp) % (sub // t)
        outs = []
        for g in range(grp):
            r0 = pl.multiple_of((i * grp + g) * t, t)
            outs.append((r0, tile(1, r0, valid_mask(n0) if g == 0 else band, None)))
        for r0, (m_new, l_new, acc_new) in outs:
            md[pl.ds(r0, t), :] = m_new
            ld[pl.ds(r0, t), :] = l_new
            accd[pl.ds(r0, t), :] = acc_new
        return 0

    lax.fori_loop(0, n_trips, d4_group, 0)

    nb16 = seq // 16 // t

    def d16_group(i, _):
        outs = []
        for g in range(grp):
            idx = i * grp + g
            res16 = idx // nb16
            n = g % nb16
            rows = pl.ds((res16 % 4) * sub + res16 // 4 + 4 * t * n, t, stride=4)
            prev = (md[rows, :], ld[rows, :], accd[rows, :])
            r0 = pl.multiple_of(idx * t, t)
            outs.append((rows, tile(2, r0, band_first if n == 0 else band, prev)))
        for rows, (m_new, l_new, acc_new) in outs:
            md[rows, :] = m_new
            ld[rows, :] = l_new
            accd[rows, :] = acc_new
        return 0

    lax.fori_loop(0, n_trips, d16_group, 0)

    def d1_group(i, _):
        outs = []
        for g in range(grp):
            n = i * grp + g
            r0 = pl.multiple_of(n * t, t)
            for r4 in range(4):
                src = pl.ds(r4 * sub + n * (t // 4), t // 4)
                dst = pl.ds(g * t + r4, t // 4, stride=4)
                mt[dst, :] = md[src, :]
                lt[dst, :] = ld[src, :]
                acct[dst, :] = accd[src, :]
            tmp = pl.ds(g * t, t)
            _, l_new, acc_new = tile(0, r0, valid_mask(n) if g == 0 else band,
                                     (mt[tmp, :], lt[tmp, :], acct[tmp, :]))
            outs.append((r0, acc_new / l_new))
        for r0, out in outs:
            o_ref[pl.ds(r0, t), :] = out.astype(o_ref.dtype)
        return 0

    lax.fori_loop(0, n_trips, d1_group, 0)


def dilated_attention(u):
    b, s, _ = u.shape
    assert s % (16 * DL_STEPS) == 0 and (s // 4 // DL_STEPS) % DL_GROUP == 0
    assert DL_GROUP % (s // 16 // DL_STEPS) == 0
    pairs = DL_HEADS // 2
    first = DL_FIRST_COL // LANES
    blk = lambda off: pl.BlockSpec((None, s, LANES), lambda i, p: (i, 0, off + p))
    nl = len(DL_DILATIONS)
    stat = pltpu.VMEM((s, LANES), F32)
    small = pltpu.VMEM((DL_GROUP * DL_STEPS, LANES), F32)
    stage = pltpu.VMEM((4 * DL_STEPS, LANES), F32)
    return pl.pallas_call(
        functools.partial(_dil_kernel, seq=s),
        grid=(b, pairs),
        in_specs=[blk(first), blk(first + pairs), blk(first + 2 * pairs)],
        out_specs=pl.BlockSpec((None, s, LANES), lambda i, p: (i, 0, p)),
        out_shape=jax.ShapeDtypeStruct((b, s, DL_HEADS * HEAD_DIM), BF16),
        scratch_shapes=[pltpu.VMEM((nl, s, LANES), BF16), pltpu.VMEM((nl, s, LANES), BF16),
                        pltpu.VMEM((nl, LANES, DL_PAD + s), BF16), pltpu.VMEM((nl, DL_PAD + s, LANES), BF16),
                        stage, stage, stage, stat, stat, stat, small, small, small],
        compiler_params=_cparams("parallel", "parallel"),
        name="dilated_attention",
    )(u, u, u)


def _post_mixer_kernel(a_ref, b_ref, c_ref, h_ref, wa_ref, wb_ref, wc_ref, gmix_ref, gpre_ref, wq_ref,
                       kv_ref, wo_ref, gpost_ref, o_ref, att_ref, *, d_model):
    y = jnp.dot(a_ref[...], wa_ref[...], preferred_element_type=F32)
    y = y + jnp.dot(b_ref[...], wb_ref[...], preferred_element_type=F32)
    y = y + jnp.dot(c_ref[...], wc_ref[...], preferred_element_type=F32)
    h = h_ref[...] + _rms(y, gmix_ref[...])

    hd = d_model // X_HEADS
    qn = _rms(h, gpre_ref[...]).astype(BF16)
    q = (jnp.dot(qn, wq_ref[...], preferred_element_type=F32) * hd ** -0.5).astype(BF16)
    for hh in range(X_HEADS):
        k = kv_ref[:, hh * hd:(hh + 1) * hd]
        v = kv_ref[:, d_model + hh * hd:d_model + (hh + 1) * hd]
        s = lax.dot_general(q[:, hh * hd:(hh + 1) * hd], k, NT_DIMS, preferred_element_type=F32)
        p = jnp.exp(s - jnp.max(s, axis=1, keepdims=True))
        den = jnp.sum(p, axis=1, keepdims=True)
        att = jnp.dot(p.astype(BF16), v, preferred_element_type=F32) / den
        att_ref[:, hh * hd:(hh + 1) * hd] = att.astype(BF16)
    y = jnp.dot(att_ref[...], wo_ref[...], preferred_element_type=F32)
    o_ref[...] = h + _rms(y, gpost_ref[...])


def post_mixer(a_out, b_out, c_out, h, w_out, g_mix_post, g_x_pre, wq, kv, wo, g_x_post, *, tm=1024):
    b, s, d = h.shape
    n_mem = kv.shape[1]
    wa_rows = a_out.shape[2]
    wb_rows = b_out.shape[2]
    rows = lambda w: pl.BlockSpec((None, tm, w), lambda i, t: (i, t, 0))
    const = lambda shape: pl.BlockSpec(shape, lambda i, t: (0, 0))
    vec = const((1, d))
    return pl.pallas_call(
        functools.partial(_post_mixer_kernel, d_model=d),
        grid=(b, s // tm),
        in_specs=[rows(wa_rows), rows(wb_rows), rows(c_out.shape[2]), rows(d),
                  const((wa_rows, d)), const((wb_rows, d)), const((c_out.shape[2], d)), vec, vec,
                  const((d, d)), pl.BlockSpec((None, n_mem, 2 * d), lambda i, t: (i, 0, 0)),
                  const((d, d)), vec],
        out_specs=rows(d),
        out_shape=jax.ShapeDtypeStruct((b, s, d), F32),
        scratch_shapes=[pltpu.VMEM((tm, d), BF16)],
        compiler_params=_cparams("parallel", "parallel"),
        name="post_mixer",
    )(a_out, b_out, c_out, h, w_out[:wa_rows], w_out[wa_rows:wa_rows + wb_rows],
      w_out[wa_rows + wb_rows:], g_mix_post.reshape(1, d), g_x_pre.reshape(1, d), wq, kv, wo,
      g_x_post.reshape(1, d))


FFN_HALO = 8


def _ffn_kernel(h_ref, halo_ref, gpre_ref, wup_ref, cw_ref, cb_ref, wdn_ref, gpost_ref, o_ref, act_ref,
                *, tm, d_ff, n_chunk):
    first = pl.program_id(1) == 0
    h = h_ref[...]
    g = gpre_ref[...]
    halo = jnp.where(first, 0.0, _rms(halo_ref[...], g))
    xn = jnp.concatenate([halo, _rms(h, g)], axis=0).astype(BF16)

    def conv(u, c0):
        y = cb_ref[:, c0:c0 + n_chunk] + u[FFN_HALO:, :] * cw_ref[FFN_KERNEL - 1:FFN_KERNEL, c0:c0 + n_chunk]
        for k in range(FFN_KERNEL - 1):
            past = pltpu.roll(u, FFN_KERNEL - 1 - k, axis=0)[FFN_HALO:, :]
            y = y + past * cw_ref[k:k + 1, c0:c0 + n_chunk]
        return y

    half = tm // 2
    for r0 in (0, half):
        xs = xn[r0:r0 + half + FFN_HALO, :]
        for c0 in range(0, d_ff, n_chunk):
            ug = jnp.dot(xs, wup_ref[:, c0:c0 + n_chunk], preferred_element_type=F32)
            uv = jnp.dot(xs, wup_ref[:, d_ff + c0:d_ff + c0 + n_chunk], preferred_element_type=F32)
            act = jax.nn.gelu(conv(ug, c0), approximate=True) * conv(uv, d_ff + c0)
            act_ref[r0:r0 + half, c0:c0 + n_chunk] = act.astype(BF16)
    for r0 in (0, half):
        y = jnp.dot(act_ref[r0:r0 + half, :], wdn_ref[...], preferred_element_type=F32)
        o_ref[r0:r0 + half, :] = h[r0:r0 + half, :] + _rms(y, gpost_ref[...])


def conv_ffn(h, g_pre, w_up, conv_w, conv_b, w_down, g_post, *, tm=512, n_chunk=256):
    b, s, d = h.shape
    d_ff = w_down.shape[0]
    hpt = tm // FFN_HALO
    const = lambda shape: pl.BlockSpec(shape, lambda i, t: (0, 0))
    return pl.pallas_call(
        functools.partial(_ffn_kernel, tm=tm, d_ff=d_ff, n_chunk=n_chunk),
        grid=(b, s // tm),
        in_specs=[pl.BlockSpec((None, tm, d), lambda i, t: (i, t, 0)),
                  pl.BlockSpec((None, FFN_HALO, d), lambda i, t: (i, jnp.maximum(t * hpt - 1, 0), 0)),
                  const((1, d)), const((d, 2 * d_ff)), const((FFN_KERNEL, 2 * d_ff)),
                  const((1, 2 * d_ff)), const((d_ff, d)), const((1, d))],
        out_specs=pl.BlockSpec((None, tm, d), lambda i, t: (i, t, 0)),
        out_shape=jax.ShapeDtypeStruct((b, s, d), F32),
        scratch_shapes=[pltpu.VMEM((tm, d_ff), BF16)],
        compiler_params=_cparams("parallel", "parallel"),
        name="conv_ffn",
    )(h, h, g_pre.reshape(1, d), w_up, conv_w, conv_b.reshape(1, 2 * d_ff), w_down,
      g_post.reshape(1, d))


def kernel(x, mem, positions, mix_norm_pre, w_in, cv_w, cv_b, cv_ln_g, cv_ln_b, cv_pw_w, cv_pw_b,
           w_out, mix_norm_post, x_norm_pre, mem_norm, x_wq, x_wk, x_wv, x_wo, x_norm_post,
           ffn_norm_pre, ffn_w_up, ffn_conv_w, ffn_conv_b, ffn_w_down, ffn_norm_post):
    b, s, d = x.shape
    n_mem = mem.shape[1]
    depth = w_in.shape[0]
    m = b * s
    tm = 1024

    cos_t, sin_t = (tab.reshape(m, LANES) for tab in rope_tables(positions))
    mem2 = mem.reshape(b * n_mem, d)
    h = x
    for l in range(depth):
        u = mixer_in_proj(h.reshape(m, d), mix_norm_pre[l], w_in[l].astype(BF16), cos_t, sin_t,
                          tm=tm).reshape(b, s, -1)
        a_out = stick_breaking(u)
        b_out = conformer_conv(u, cv_w[l], cv_b[l], cv_ln_g[l], cv_ln_b[l],
                               cv_pw_w[l].astype(BF16), cv_pw_b[l])
        c_out = dilated_attention(u)
        wkv = jnp.concatenate([x_wk[l], x_wv[l]], axis=1).astype(BF16)
        kv = norm_matmul(mem2, mem_norm[l], wkv, out_dtype=BF16, tm=tm).reshape(b, n_mem, 2 * d)
        h = post_mixer(a_out, b_out, c_out, h, w_out[l].astype(BF16), mix_norm_post[l], x_norm_pre[l],
                       x_wq[l].astype(BF16), kv, x_wo[l].astype(BF16), x_norm_post[l])
        h = conv_ffn(h, ffn_norm_pre[l], ffn_w_up[l].astype(BF16), ffn_conv_w[l],
                     ffn_conv_b[l], ffn_w_down[l].astype(BF16), ffn_norm_post[l])
    return h
```

```python
import functools

import jax
import jax.numpy as jnp
from jax import lax
from jax.experimental import pallas as pl
from jax.experimental.pallas import tpu as pltpu

F32 = jnp.float32
BF16 = jnp.bfloat16

EPS = 1e-6
HEAD_DIM = 64
LANES = 128
SB_HEADS = 4
CV_WIDTH = 256
CV_KERNEL = 31
DL_HEADS = 8
DL_STEPS = 128
ROPE_THETA = 10000.0
X_HEADS = 4
FFN_KERNEL = 3

VMEM_LIMIT = 56 * 1024 * 1024

NT_DIMS = (((1,), (1,)), ((), ()))


def _cparams(*sem):
    return pltpu.CompilerParams(dimension_semantics=sem, vmem_limit_bytes=VMEM_LIMIT)


def _rms(x, g):
    ms = jnp.mean(x * x, axis=-1, keepdims=True)
    return x * lax.rsqrt(ms + EPS) * g


def _layer_weight(w, layer):
    return pl.BlockSpec((None,) + w.shape[1:], lambda *_: (layer, 0, 0))


def _norm_matmul_kernel(x_ref, g_ref, w_ref, o_ref, *, n_chunk):
    xn = _rms(x_ref[...], g_ref[...]).astype(BF16)
    n = w_ref.shape[1]
    for c in range(0, n, n_chunk):
        y = jnp.dot(xn, w_ref[:, c:c + n_chunk], preferred_element_type=F32)
        o_ref[:, c:c + n_chunk] = y.astype(o_ref.dtype)


def norm_matmul(x, g, w, layer, *, out_dtype, tm, n_chunk=256):
    m, k = x.shape
    n = w.shape[2]
    return pl.pallas_call(
        functools.partial(_norm_matmul_kernel, n_chunk=n_chunk),
        grid=(m // tm,),
        in_specs=[pl.BlockSpec((tm, k), lambda i: (i, 0)),
                  pl.BlockSpec((1, k), lambda i: (0, 0)),
                  _layer_weight(w, layer)],
        out_specs=pl.BlockSpec((tm, n), lambda i: (i, 0)),
        out_shape=jax.ShapeDtypeStruct((m, n), out_dtype),
        compiler_params=_cparams("parallel"),
        name="norm_matmul",
    )(x, g.reshape(1, k), w)


DL_FIRST_COL = 3 * SB_HEADS * HEAD_DIM + 2 * CV_WIDTH
DL_WIDTH = DL_HEADS * HEAD_DIM


def _rope(x, cos, sin):
    ln = lax.broadcasted_iota(jnp.int32, x.shape, 1)
    first_half = (ln % HEAD_DIM) < (HEAD_DIM // 2)
    partner = jnp.where(first_half, pltpu.roll(x, LANES - HEAD_DIM // 2, axis=1),
                        pltpu.roll(x, HEAD_DIM // 2, axis=1))
    return x * cos + partner * sin


def _in_proj_kernel(x_ref, g_ref, w_ref, cos_ref, sin_ref, o_ref, *, n_chunk):
    xn = _rms(x_ref[...], g_ref[...]).astype(BF16)
    cos = cos_ref[...]
    sin = sin_ref[...]
    k_col = DL_FIRST_COL + DL_WIDTH
    v_col = k_col + DL_WIDTH
    for c in range(0, w_ref.shape[1], n_chunk):
        y = jnp.dot(xn, w_ref[:, c:c + n_chunk], preferred_element_type=F32)
        for col in range(c, c + n_chunk, LANES):
            blk = y[:, col - c:col - c + LANES]
            if DL_FIRST_COL <= col < k_col:
                blk = _rope(blk, cos, sin) * HEAD_DIM ** -0.5
            elif k_col <= col < v_col:
                blk = _rope(blk, cos, sin)
            o_ref[:, col:col + LANES] = blk


def mixer_in_proj(x, g, w, layer, cos_t, sin_t, *, tm, n_chunk=256):
    m, k = x.shape
    n = w.shape[2]
    return pl.pallas_call(
        functools.partial(_in_proj_kernel, n_chunk=n_chunk),
        grid=(m // tm,),
        in_specs=[pl.BlockSpec((tm, k), lambda i: (i, 0)),
                  pl.BlockSpec((1, k), lambda i: (0, 0)),
                  _layer_weight(w, layer),
                  pl.BlockSpec((tm, LANES), lambda i: (i, 0)),
                  pl.BlockSpec((tm, LANES), lambda i: (i, 0))],
        out_specs=pl.BlockSpec((tm, n), lambda i: (i, 0)),
        out_shape=jax.ShapeDtypeStruct((m, n), F32),
        compiler_params=_cparams("parallel"),
        name="mixer_in_proj",
    )(x, g.reshape(1, k), w, cos_t, sin_t)


SB_MASKED_Z = -1e4
SB_DEAD = -104.0


def _sb_kernel(q_ref, k_ref, v_ref, o_ref, kt_ref, vb_ref, *, seq, tq, n_fast, q_group):
    scale = HEAD_DIM ** -0.5
    vb_ref[...] = v_ref[...].astype(BF16)

    def transpose_keys(c, _):
        c0 = pl.multiple_of(c * LANES, LANES)
        kt_ref[:, pl.ds(c0, LANES)] = k_ref[pl.ds(c0, LANES), :].T.astype(BF16)
        return 0

    lax.fori_loop(0, seq // LANES, transpose_keys, 0, unroll=4)

    head0 = lax.broadcasted_iota(jnp.int32, (tq, LANES), 1) < HEAD_DIM
    row = lax.broadcasted_iota(jnp.int32, (2 * tq, tq), 0) % tq
    col = lax.broadcasted_iota(jnp.int32, (2 * tq, tq), 1)
    before = col < row
    krow = lax.broadcasted_iota(jnp.int32, (tq, tq), 0)
    kcol = lax.broadcasted_iota(jnp.int32, (tq, tq), 1)
    later = jnp.where(krow > kcol, 1.0, 0.0).astype(BF16)
    later2 = jnp.concatenate([later, later], axis=0)

    def tile(q2, kt_blk, vblk, carry, ok):
        z = jnp.dot(q2, kt_blk, preferred_element_type=F32)
        if ok is not None:
            z = jnp.where(ok, z, SB_MASKED_Z)
        softplus = jnp.log(1.0 + jnp.exp(-jnp.abs(z)))
        log_beta = jnp.minimum(z, 0.0) - softplus
        log_keep = log_beta - z
        hi = log_keep.astype(BF16)
        lo = (log_keep - hi.astype(F32)).astype(BF16)
        between = jnp.dot(jnp.concatenate([hi, lo], axis=1), later2,
                          preferred_element_type=F32) + carry
        a = jnp.exp(log_beta + between)
        pv = jnp.dot(a.astype(BF16), vblk, preferred_element_type=F32)
        return pv, carry + jnp.sum(log_keep, axis=1, keepdims=True)

    def near_blocks(qb):
        q0 = pl.multiple_of(qb * tq, tq)
        q = q_ref[pl.ds(q0, tq), :] * scale
        q2 = jnp.concatenate([jnp.where(head0, q, 0.0), jnp.where(head0, 0.0, q)], axis=0).astype(BF16)
        acc = jnp.zeros((2 * tq, LANES), F32)
        carry = jnp.zeros((2 * tq, 1), F32)
        for j in range(n_fast):
            kblock = qb - j
            k0 = pl.multiple_of(jnp.maximum(kblock, 0) * tq, tq)
            ok = before if j == 0 else kblock >= 0
            pv, carry = tile(q2, kt_ref[:, pl.ds(k0, tq)], vb_ref[pl.ds(k0, tq), :], carry, ok)
            acc = acc + pv
        return q2, acc, carry

    def far_blocks(qb, q2, acc, carry):
        def alive(st):
            kblock, _, c = st
            return (kblock >= 0) & (jnp.max(c) > SB_DEAD)

        def k_body(st):
            kblock, a, c = st
            k0 = pl.multiple_of(kblock * tq, tq)
            pv, c = tile(q2, kt_ref[:, pl.ds(k0, tq)], vb_ref[pl.ds(k0, tq), :], c, None)
            return kblock - 1, a + pv, c

        return lax.while_loop(alive, k_body, (qb - n_fast, acc, carry))[1]

    def q_body(i, _):
        started = [(i * q_group + g,) + near_blocks(i * q_group + g) for g in range(q_group)]
        for qb, q2, acc, carry in started:
            acc = far_blocks(qb, q2, acc, carry)
            q0 = pl.multiple_of(qb * tq, tq)
            o_ref[pl.ds(q0, tq), :] = jnp.where(head0, acc[:tq], acc[tq:]).astype(o_ref.dtype)
        return 0

    lax.fori_loop(0, seq // tq // q_group, q_body, 0)


def stick_breaking(u, *, tq=256, n_fast=2, q_group=4):
    b, s, _ = u.shape
    pairs = SB_HEADS // 2
    blk = lambda off: pl.BlockSpec((None, s, LANES), lambda i, p: (i, 0, off + p))
    return pl.pallas_call(
        functools.partial(_sb_kernel, seq=s, tq=tq, n_fast=n_fast, q_group=q_group),
        grid=(b, pairs),
        in_specs=[blk(0), blk(pairs), blk(2 * pairs)],
        out_specs=pl.BlockSpec((None, s, LANES), lambda i, p: (i, 0, p)),
        out_shape=jax.ShapeDtypeStruct((b, s, SB_HEADS * HEAD_DIM), BF16),
        scratch_shapes=[pltpu.VMEM((LANES, s), BF16), pltpu.VMEM((s, LANES), BF16)],
        compiler_params=_cparams("parallel", "parallel"),
        name="stick_breaking",
    )(u, u, u)


CV_HALO = 32


def _conv_kernel(val_ref, gate_ref, hval_ref, hgate_ref, w_ref, b_ref, lng_ref, lnb_ref,
                 pw_ref, pwb_ref, o_ref, g_scr, *, tm):
    first = pl.program_id(1) == 0
    g_scr[CV_HALO:, :] = val_ref[...] * jax.nn.sigmoid(gate_ref[...])
    halo = hval_ref[...] * jax.nn.sigmoid(hgate_ref[...])
    g_scr[:CV_HALO, :] = jnp.where(first, 0.0, halo)
    w = w_ref[...]
    g_ext = g_scr[...]
    acc = jnp.zeros((tm, CV_WIDTH), F32) + b_ref[...]
    rolled = {}
    for k in range(CV_KERNEL):
        lo = CV_HALO - (CV_KERNEL - 1) + k
        start = -(-lo // 8) * 8
        shift = start - lo
        if shift not in rolled:
            rolled[shift] = pltpu.roll(g_ext, shift, axis=0) if shift else g_ext
        acc = acc + rolled[shift][start:start + tm, :] * w[k:k + 1, :]
    mu = jnp.mean(acc, axis=-1, keepdims=True)
    xc = acc - mu
    var = jnp.mean(xc * xc, axis=-1, keepdims=True)
    y = xc * lax.rsqrt(var + EPS) * lng_ref[...] + lnb_ref[...]
    y = y * jax.nn.sigmoid(y)
    out = jnp.dot(y.astype(BF16), pw_ref[...], preferred_element_type=F32) + pwb_ref[...]
    o_ref[...] = out.astype(o_ref.dtype)


def conformer_conv(u, cv_w, cv_b, ln_g, ln_b, pw_w, layer, pw_b, *, tm=1024):
    b, s, _ = u.shape
    c = CV_WIDTH
    val_blk = 3 * SB_HEADS * HEAD_DIM // c
    hpt = tm // CV_HALO
    cur = lambda off: pl.BlockSpec((None, tm, c), lambda i, t: (i, t, off))
    halo = lambda off: pl.BlockSpec((None, CV_HALO, c),
                                    lambda i, t: (i, jnp.maximum(t * hpt - 1, 0), off))
    vec = pl.BlockSpec((1, c), lambda i, t: (0, 0))
    return pl.pallas_call(
        functools.partial(_conv_kernel, tm=tm),
        grid=(b, s // tm),
        in_specs=[cur(val_blk), cur(val_blk + 1), halo(val_blk), halo(val_blk + 1),
                  pl.BlockSpec((CV_KERNEL, c), lambda i, t: (0, 0)), vec, vec, vec,
                  _layer_weight(pw_w, layer), vec],
        out_specs=pl.BlockSpec((None, tm, c), lambda i, t: (i, t, 0)),
        out_shape=jax.ShapeDtypeStruct((b, s, c), BF16),
        scratch_shapes=[pltpu.VMEM((tm + CV_HALO, c), F32)],
        compiler_params=_cparams("parallel", "parallel"),
        name="conformer_conv",
    )(u, u, u, u, cv_w, cv_b.reshape(1, c), ln_g.reshape(1, c), ln_b.reshape(1, c),
      pw_w, pw_b.reshape(1, c))


def _rope_table_kernel(pos_ref, invf_ref, cos_ref, sin_ref, *, seq):
    half = HEAD_DIM // 2
    ang = invf_ref[...] * pos_ref[...].astype(F32)
    c = jnp.cos(ang)
    s = jnp.sin(ang)
    cos_t = jnp.concatenate([c, c, c, c], axis=0)
    sin_t = jnp.concatenate([-s, s, -s, s], axis=0)
    del half
    for j in range(seq // LANES):
        cos_ref[j * LANES:(j + 1) * LANES, :] = cos_t[:, j * LANES:(j + 1) * LANES].T
        sin_ref[j * LANES:(j + 1) * LANES, :] = sin_t[:, j * LANES:(j + 1) * LANES].T


def rope_tables(positions):
    b, s = positions.shape
    half = HEAD_DIM // 2
    inv_freq = ROPE_THETA ** (-jnp.arange(half, dtype=F32) / half)
    out = jax.ShapeDtypeStruct((b, s, LANES), F32)
    return pl.pallas_call(
        functools.partial(_rope_table_kernel, seq=s),
        grid=(b,),
        in_specs=[pl.BlockSpec((None, 1, s), lambda i: (i, 0, 0)),
                  pl.BlockSpec((half, 1), lambda i: (0, 0))],
        out_specs=[pl.BlockSpec((None, s, LANES), lambda i: (i, 0, 0))] * 2,
        out_shape=[out, out],
        compiler_params=_cparams("parallel"),
        name="rope_tables",
    )(positions.reshape(b, 1, s), inv_freq.reshape(half, 1))


DL_DILATIONS = (1, 4, 16)
DL_GROUP = 8
DL_PAD = DL_STEPS


def _dil_kernel(q_ref, k_ref, v_ref, o_ref, q0s, q1s, kts, vs, stq, stk, stv, accd, md, ld, acct, mt, lt,
                *, seq):
    t = DL_STEPS
    sub = seq // 4
    head0 = lax.broadcasted_iota(jnp.int32, (t, LANES), 1) < HEAD_DIM
    qi = lax.broadcasted_iota(jnp.int32, (2 * t, 2 * t), 0) % t
    kj = lax.broadcasted_iota(jnp.int32, (2 * t, 2 * t), 1)
    band = (kj >= qi) & (kj <= qi + t)
    band_first = band & (kj >= t)

    def valid_mask(n):
        return band & (kj >= jnp.where(n == 0, t, 0))

    def emit(li, dst0, q, k, v):
        q0s[li, pl.ds(dst0, t), :] = jnp.where(head0, q, 0.0).astype(BF16)
        q1s[li, pl.ds(dst0, t), :] = jnp.where(head0, 0.0, q).astype(BF16)
        kts[li, :, pl.ds(DL_PAD + dst0, t)] = k.T.astype(BF16)
        vs[li, pl.ds(DL_PAD + dst0, t), :] = v.astype(BF16)

    for li in range(len(DL_DILATIONS)):
        kts[li, :, :DL_PAD] = jnp.zeros((LANES, DL_PAD), BF16)
        vs[li, :DL_PAD, :] = jnp.zeros((DL_PAD, LANES), BF16)

    def nat_body(c, _):
        rows = pl.ds(pl.multiple_of(c * t, t), t)
        emit(0, pl.multiple_of(c * t, t), q_ref[rows, :], k_ref[rows, :], v_ref[rows, :])
        return 0

    lax.fori_loop(0, seq // t, nat_body, 0, unroll=4)

    blocks = sub // (4 * t)

    def d4_d16_body(c, _):
        res4 = c // blocks
        m0 = (c % blocks) * 4 * t
        src = pl.ds(res4 + 4 * m0, 4 * t, stride=4)
        stq[...] = q_ref[src, :]
        stk[...] = k_ref[src, :]
        stv[...] = v_ref[src, :]
        for part in range(4):
            rows = pl.ds(part * t, t)
            emit(1, pl.multiple_of(res4 * sub + m0 + part * t, t), stq[rows, :], stk[rows, :], stv[rows, :])
        for s4 in range(4):
            rows = pl.ds(s4, t, stride=4)
            dst0 = pl.multiple_of((res4 + 4 * s4) * (seq // 16) + m0 // 4, t)
            emit(2, dst0, stq[rows, :], stk[rows, :], stv[rows, :])
        return 0

    lax.fori_loop(0, 4 * blocks, d4_d16_body, 0)

    def tile(li, r0, valid, prev):
        q2 = jnp.concatenate([q0s[li, pl.ds(r0, t), :], q1s[li, pl.ds(r0, t), :]], axis=0)
        kt = kts[li, :, pl.ds(r0, 2 * t)]
        vw = vs[li, pl.ds(r0, 2 * t), :]
        s = jnp.where(valid, jnp.dot(q2, kt, preferred_element_type=F32), -jnp.inf)
        m2 = jnp.max(s, axis=1, keepdims=True)
        p = jnp.exp(s - m2)
        l2 = jnp.sum(p, axis=1, keepdims=True)
        pv = jnp.dot(p.astype(BF16), vw, preferred_element_type=F32)
        m_new = jnp.where(head0, m2[:t], m2[t:])
        l_new = jnp.where(head0, l2[:t], l2[t:])
        acc_new = jnp.where(head0, pv[:t], pv[t:])
        if prev is not None:
            m_prev, l_prev, acc_prev = prev
            m_tile = m_new
            m_new = jnp.maximum(m_prev, m_tile)
            a_prev = jnp.exp(m_prev - m_new)
            a_tile = jnp.exp(m_tile - m_new)
            l_new = a_prev * l_prev + a_tile * l_new
            acc_new = a_prev * acc_prev + a_tile * acc_new
        return m_new, l_new, acc_new

    grp = DL_GROUP
    n_trips = seq // t // grp

    def d4_group(i, _):
        n0 = (i * grp) % (sub // t)
        outs = []
        for g in range(grp):
            r0 = pl.multiple_of((i * grp + g) * t, t)
            outs.append((r0, tile(1, r0, valid_mask(n0) if g == 0 else band, None)))
        for r0, (m_new, l_new, acc_new) in outs:
            md[pl.ds(r0, t), :] = m_new
            ld[pl.ds(r0, t), :] = l_new
            accd[pl.ds(r0, t), :] = acc_new
        return 0

    lax.fori_loop(0, n_trips, d4_group, 0)

    nb16 = seq // 16 // t

    def d16_group(i, _):
        outs = []
        for g in range(grp):
            idx = i * grp + g
            res16 = idx // nb16
            n = g % nb16
            rows = pl.ds((res16 % 4) * sub + res16 // 4 + 4 * t * n, t, stride=4)
            prev = (md[rows, :], ld[rows, :], accd[rows, :])
            r0 = pl.multiple_of(idx * t, t)
            outs.append((rows, tile(2, r0, band_first if n == 0 else band, prev)))
        for rows, (m_new, l_new, acc_new) in outs:
            md[rows, :] = m_new
            ld[rows, :] = l_new
            accd[rows, :] = acc_new
        return 0

    lax.fori_loop(0, n_trips, d16_group, 0)

    def d1_group(i, _):
        outs = []
        for g in range(grp):
            n = i * grp + g
            r0 = pl.multiple_of(n * t, t)
            for r4 in range(4):
                src = pl.ds(r4 * sub + n * (t // 4), t // 4)
                dst = pl.ds(g * t + r4, t // 4, stride=4)
                mt[dst, :] = md[src, :]
                lt[dst, :] = ld[src, :]
                acct[dst, :] = accd[src, :]
            tmp = pl.ds(g * t, t)
            _, l_new, acc_new = tile(0, r0, valid_mask(n) if g == 0 else band,
                                     (mt[tmp, :], lt[tmp, :], acct[tmp, :]))
            outs.append((r0, acc_new / l_new))
        for r0, out in outs:
            o_ref[pl.ds(r0, t), :] = out.astype(o_ref.dtype)
        return 0

    lax.fori_loop(0, n_trips, d1_group, 0)


def dilated_attention(u):
    b, s, _ = u.shape
    assert s % (16 * DL_STEPS) == 0 and (s // 4 // DL_STEPS) % DL_GROUP == 0
    assert DL_GROUP % (s // 16 // DL_STEPS) == 0
    pairs = DL_HEADS // 2
    first = DL_FIRST_COL // LANES
    blk = lambda off: pl.BlockSpec((None, s, LANES), lambda i, p: (i, 0, off + p))
    nl = len(DL_DILATIONS)
    stat = pltpu.VMEM((s, LANES), F32)
    small = pltpu.VMEM((DL_GROUP * DL_STEPS, LANES), F32)
    stage = pltpu.VMEM((4 * DL_STEPS, LANES), F32)
    return pl.pallas_call(
        functools.partial(_dil_kernel, seq=s),
        grid=(b, pairs),
        in_specs=[blk(first), blk(first + pairs), blk(first + 2 * pairs)],
        out_specs=pl.BlockSpec((None, s, LANES), lambda i, p: (i, 0, p)),
        out_shape=jax.ShapeDtypeStruct((b, s, DL_HEADS * HEAD_DIM), BF16),
        scratch_shapes=[pltpu.VMEM((nl, s, LANES), BF16), pltpu.VMEM((nl, s, LANES), BF16),
                        pltpu.VMEM((nl, LANES, DL_PAD + s), BF16), pltpu.VMEM((nl, DL_PAD + s, LANES), BF16),
                        stage, stage, stage, stat, stat, stat, small, small, small],
        compiler_params=_cparams("parallel", "parallel"),
        name="dilated_attention",
    )(u, u, u)


def _post_mixer_kernel(a_ref, b_ref, c_ref, h_ref, wout_ref, gmix_ref, gpre_ref, wq_ref,
                       kv_ref, wo_ref, gpost_ref, o_ref, att_ref, *, d_model):
    b_row = a_ref.shape[1]
    c_row = b_row + b_ref.shape[1]
    y = jnp.dot(a_ref[...], wout_ref[:b_row, :], preferred_element_type=F32)
    y = y + jnp.dot(b_ref[...], wout_ref[b_row:c_row, :], preferred_element_type=F32)
    y = y + jnp.dot(c_ref[...], wout_ref[c_row:, :], preferred_element_type=F32)
    h = h_ref[...] + _rms(y, gmix_ref[...])

    hd = d_model // X_HEADS
    qn = _rms(h, gpre_ref[...]).astype(BF16)
    q = (jnp.dot(qn, wq_ref[...], preferred_element_type=F32) * hd ** -0.5).astype(BF16)
    for hh in range(X_HEADS):
        k = kv_ref[:, hh * hd:(hh + 1) * hd]
        v = kv_ref[:, d_model + hh * hd:d_model + (hh + 1) * hd]
        s = lax.dot_general(q[:, hh * hd:(hh + 1) * hd], k, NT_DIMS, preferred_element_type=F32)
        p = jnp.exp(s - jnp.max(s, axis=1, keepdims=True))
        den = jnp.sum(p, axis=1, keepdims=True)
        att = jnp.dot(p.astype(BF16), v, preferred_element_type=F32) / den
        att_ref[:, hh * hd:(hh + 1) * hd] = att.astype(BF16)
    y = jnp.dot(att_ref[...], wo_ref[...], preferred_element_type=F32)
    o_ref[...] = h + _rms(y, gpost_ref[...])


def post_mixer(a_out, b_out, c_out, h, w_out, g_mix_post, g_x_pre, wq, kv, wo, g_x_post, layer, *, tm=1024):
    b, s, d = h.shape
    n_mem = kv.shape[1]
    rows = lambda w: pl.BlockSpec((None, tm, w), lambda i, t: (i, t, 0))
    vec = pl.BlockSpec((1, d), lambda i, t: (0, 0))
    return pl.pallas_call(
        functools.partial(_post_mixer_kernel, d_model=d),
        grid=(b, s // tm),
        in_specs=[rows(a_out.shape[2]), rows(b_out.shape[2]), rows(c_out.shape[2]), rows(d),
                  _layer_weight(w_out, layer), vec, vec,
                  _layer_weight(wq, layer), pl.BlockSpec((None, n_mem, 2 * d), lambda i, t: (i, 0, 0)),
                  _layer_weight(wo, layer), vec],
        out_specs=rows(d),
        out_shape=jax.ShapeDtypeStruct((b, s, d), F32),
        scratch_shapes=[pltpu.VMEM((tm, d), BF16)],
        compiler_params=_cparams("parallel", "parallel"),
        name="post_mixer",
    )(a_out, b_out, c_out, h, w_out, g_mix_post.reshape(1, d), g_x_pre.reshape(1, d), wq, kv, wo,
      g_x_post.reshape(1, d))


FFN_HALO = 8


def _ffn_kernel(h_ref, halo_ref, gpre_ref, wup_ref, cw_ref, cb_ref, wdn_ref, gpost_ref, o_ref, act_ref,
                *, tm, d_ff, n_chunk):
    first = pl.program_id(1) == 0
    h = h_ref[...]
    g = gpre_ref[...]
    halo = jnp.where(first, 0.0, _rms(halo_ref[...], g))
    xn = jnp.concatenate([halo, _rms(h, g)], axis=0).astype(BF16)

    def conv(u, c0):
        y = cb_ref[:, c0:c0 + n_chunk] + u[FFN_HALO:, :] * cw_ref[FFN_KERNEL - 1:FFN_KERNEL, c0:c0 + n_chunk]
        for k in range(FFN_KERNEL - 1):
            past = pltpu.roll(u, FFN_KERNEL - 1 - k, axis=0)[FFN_HALO:, :]
            y = y + past * cw_ref[k:k + 1, c0:c0 + n_chunk]
        return y

    half = tm // 2
    for r0 in (0, half):
        xs = xn[r0:r0 + half + FFN_HALO, :]
        for c0 in range(0, d_ff, n_chunk):
            ug = jnp.dot(xs, wup_ref[:, c0:c0 + n_chunk], preferred_element_type=F32)
            uv = jnp.dot(xs, wup_ref[:, d_ff + c0:d_ff + c0 + n_chunk], preferred_element_type=F32)
            act = jax.nn.gelu(conv(ug, c0), approximate=True) * conv(uv, d_ff + c0)
            act_ref[r0:r0 + half, c0:c0 + n_chunk] = act.astype(BF16)
    for r0 in (0, half):
        y = jnp.dot(act_ref[r0:r0 + half, :], wdn_ref[...], preferred_element_type=F32)
        o_ref[r0:r0 + half, :] = h[r0:r0 + half, :] + _rms(y, gpost_ref[...])


def conv_ffn(h, g_pre, w_up, conv_w, conv_b, w_down, g_post, layer, *, tm=512, n_chunk=256):
    b, s, d = h.shape
    d_ff = w_down.shape[1]
    hpt = tm // FFN_HALO
    const = lambda shape: pl.BlockSpec(shape, lambda i, t: (0, 0))
    return pl.pallas_call(
        functools.partial(_ffn_kernel, tm=tm, d_ff=d_ff, n_chunk=n_chunk),
        grid=(b, s // tm),
        in_specs=[pl.BlockSpec((None, tm, d), lambda i, t: (i, t, 0)),
                  pl.BlockSpec((None, FFN_HALO, d), lambda i, t: (i, jnp.maximum(t * hpt - 1, 0), 0)),
                  const((1, d)), _layer_weight(w_up, layer), const((FFN_KERNEL, 2 * d_ff)),
                  const((1, 2 * d_ff)), _layer_weight(w_down, layer), const((1, d))],
        out_specs=pl.BlockSpec((None, tm, d), lambda i, t: (i, t, 0)),
        out_shape=jax.ShapeDtypeStruct((b, s, d), F32),
        scratch_shapes=[pltpu.VMEM((tm, d_ff), BF16)],
        compiler_params=_cparams("parallel", "parallel"),
        name="conv_ffn",
    )(h, h, g_pre.reshape(1, d), w_up, conv_w, conv_b.reshape(1, 2 * d_ff), w_down,
      g_post.reshape(1, d))


def kernel(x, mem, positions, mix_norm_pre, w_in, cv_w, cv_b, cv_ln_g, cv_ln_b, cv_pw_w, cv_pw_b,
           w_out, mix_norm_post, x_norm_pre, mem_norm, x_wq, x_wk, x_wv, x_wo, x_norm_post,
           ffn_norm_pre, ffn_w_up, ffn_conv_w, ffn_conv_b, ffn_w_down, ffn_norm_post):
    b, s, d = x.shape
    n_mem = mem.shape[1]
    depth = w_in.shape[0]
    m = b * s
    tm = 1024

    cos_t, sin_t = (tab.reshape(m, LANES) for tab in rope_tables(positions))
    mem2 = mem.reshape(b * n_mem, d)
    w_in, cv_pw_w, w_out, x_wq, x_wo, ffn_w_up, ffn_w_down = (
        w.astype(BF16) for w in (w_in, cv_pw_w, w_out, x_wq, x_wo, ffn_w_up, ffn_w_down))
    x_wkv = jnp.concatenate([x_wk, x_wv], axis=2).astype(BF16)
    h = x
    for l in range(depth):
        u = mixer_in_proj(h.reshape(m, d), mix_norm_pre[l], w_in, l, cos_t, sin_t, tm=tm).reshape(b, s, -1)
        a_out = stick_breaking(u)
        b_out = conformer_conv(u, cv_w[l], cv_b[l], cv_ln_g[l], cv_ln_b[l], cv_pw_w, l, cv_pw_b[l])
        c_out = dilated_attention(u)
        kv = norm_matmul(mem2, mem_norm[l], x_wkv, l, out_dtype=BF16, tm=tm).reshape(b, n_mem, 2 * d)
        h = post_mixer(a_out, b_out, c_out, h, w_out, mix_norm_post[l], x_norm_pre[l],
                       x_wq, kv, x_wo, x_norm_post[l], l)
        h = conv_ffn(h, ffn_norm_pre[l], ffn_w_up, ffn_conv_w[l], ffn_conv_b[l], ffn_w_down,
                     ffn_norm_post[l], l)
    return h
```

```python
import functools

import jax
import jax.numpy as jnp
from jax import lax
from jax.experimental import pallas as pl
from jax.experimental.pallas import tpu as pltpu

F32 = jnp.float32
BF16 = jnp.bfloat16

EPS = 1e-6
HEAD_DIM = 64
LANES = 128
SB_HEADS = 4
CV_WIDTH = 256
CV_KERNEL = 31
DL_HEADS = 8
DL_STEPS = 128
ROPE_THETA = 10000.0
X_HEADS = 4
FFN_KERNEL = 3

VMEM_LIMIT = 56 * 1024 * 1024

NT_DIMS = (((1,), (1,)), ((), ()))


def _cparams(*sem):
    return pltpu.CompilerParams(dimension_semantics=sem, vmem_limit_bytes=VMEM_LIMIT)


def _rms(x, g):
    ms = jnp.mean(x * x, axis=-1, keepdims=True)
    return x * lax.rsqrt(ms + EPS) * g


def _layer_weight(w, layer):
    return pl.BlockSpec((None,) + w.shape[1:], lambda *_: (layer, 0, 0))


def _norm_matmul_kernel(x_ref, g_ref, w_ref, o_ref, *, n_chunk):
    xn = _rms(x_ref[...], g_ref[...]).astype(BF16)
    n = w_ref.shape[1]
    for c in range(0, n, n_chunk):
        y = jnp.dot(xn, w_ref[:, c:c + n_chunk], preferred_element_type=F32)
        o_ref[:, c:c + n_chunk] = y.astype(o_ref.dtype)


def norm_matmul(x, g, w, layer, *, out_dtype, tm, n_chunk=256):
    m, k = x.shape
    n = w.shape[2]
    return pl.pallas_call(
        functools.partial(_norm_matmul_kernel, n_chunk=n_chunk),
        grid=(m // tm,),
        in_specs=[pl.BlockSpec((tm, k), lambda i: (i, 0)),
                  pl.BlockSpec((1, k), lambda i: (0, 0)),
                  _layer_weight(w, layer)],
        out_specs=pl.BlockSpec((tm, n), lambda i: (i, 0)),
        out_shape=jax.ShapeDtypeStruct((m, n), out_dtype),
        compiler_params=_cparams("parallel"),
        name="norm_matmul",
    )(x, g.reshape(1, k), w)


DL_FIRST_COL = 3 * SB_HEADS * HEAD_DIM + 2 * CV_WIDTH
DL_WIDTH = DL_HEADS * HEAD_DIM


def _rope(x, cos, sin):
    ln = lax.broadcasted_iota(jnp.int32, x.shape, 1)
    first_half = (ln % HEAD_DIM) < (HEAD_DIM // 2)
    partner = jnp.where(first_half, pltpu.roll(x, LANES - HEAD_DIM // 2, axis=1),
                        pltpu.roll(x, HEAD_DIM // 2, axis=1))
    return x * cos + partner * sin


def _in_proj_kernel(x_ref, g_ref, w_ref, cos_ref, sin_ref, o_ref, *, n_chunk):
    xn = _rms(x_ref[...], g_ref[...]).astype(BF16)
    cos = cos_ref[...]
    sin = sin_ref[...]
    k_col = DL_FIRST_COL + DL_WIDTH
    v_col = k_col + DL_WIDTH
    for c in range(0, w_ref.shape[1], n_chunk):
        y = jnp.dot(xn, w_ref[:, c:c + n_chunk], preferred_element_type=F32)
        for col in range(c, c + n_chunk, LANES):
            blk = y[:, col - c:col - c + LANES]
            if DL_FIRST_COL <= col < k_col:
                blk = _rope(blk, cos, sin) * HEAD_DIM ** -0.5
            elif k_col <= col < v_col:
                blk = _rope(blk, cos, sin)
            o_ref[:, col:col + LANES] = blk


def mixer_in_proj(x, g, w, layer, cos_t, sin_t, *, tm, n_chunk=256):
    m, k = x.shape
    n = w.shape[2]
    return pl.pallas_call(
        functools.partial(_in_proj_kernel, n_chunk=n_chunk),
        grid=(m // tm,),
        in_specs=[pl.BlockSpec((tm, k), lambda i: (i, 0)),
                  pl.BlockSpec((1, k), lambda i: (0, 0)),
                  _layer_weight(w, layer),
                  pl.BlockSpec((tm, LANES), lambda i: (i, 0)),
                  pl.BlockSpec((tm, LANES), lambda i: (i, 0))],
        out_specs=pl.BlockSpec((tm, n), lambda i: (i, 0)),
        out_shape=jax.ShapeDtypeStruct((m, n), F32),
        compiler_params=_cparams("parallel"),
        name="mixer_in_proj",
    )(x, g.reshape(1, k), w, cos_t, sin_t)


SB_MASKED_Z = -1e4
SB_DEAD = -104.0


def _sb_kernel(q_ref, k_ref, v_ref, o_ref, kt_ref, vb_ref, *, seq, tq, n_fast, q_group):
    scale = HEAD_DIM ** -0.5
    vb_ref[...] = v_ref[...].astype(BF16)

    def transpose_keys(c, _):
        c0 = pl.multiple_of(c * LANES, LANES)
        kt_ref[:, pl.ds(c0, LANES)] = k_ref[pl.ds(c0, LANES), :].T.astype(BF16)
        return 0

    lax.fori_loop(0, seq // LANES, transpose_keys, 0, unroll=4)

    head0 = lax.broadcasted_iota(jnp.int32, (tq, LANES), 1) < HEAD_DIM
    row = lax.broadcasted_iota(jnp.int32, (2 * tq, tq), 0) % tq
    col = lax.broadcasted_iota(jnp.int32, (2 * tq, tq), 1)
    before = col < row
    krow = lax.broadcasted_iota(jnp.int32, (tq, tq), 0)
    kcol = lax.broadcasted_iota(jnp.int32, (tq, tq), 1)
    later = jnp.where(krow > kcol, 1.0, 0.0).astype(BF16)
    later2 = jnp.concatenate([later, later], axis=0)

    def tile(q2, kt_blk, vblk, carry, ok):
        z = jnp.dot(q2, kt_blk, preferred_element_type=F32)
        if ok is not None:
            z = jnp.where(ok, z, SB_MASKED_Z)
        softplus = jnp.log(1.0 + jnp.exp(-jnp.abs(z)))
        log_beta = jnp.minimum(z, 0.0) - softplus
        log_keep = log_beta - z
        hi = log_keep.astype(BF16)
        lo = (log_keep - hi.astype(F32)).astype(BF16)
        between = jnp.dot(jnp.concatenate([hi, lo], axis=1), later2,
                          preferred_element_type=F32) + carry
        a = jnp.exp(log_beta + between)
        pv = jnp.dot(a.astype(BF16), vblk, preferred_element_type=F32)
        return pv, carry + jnp.sum(log_keep, axis=1, keepdims=True)

    def near_blocks(qb):
        q0 = pl.multiple_of(qb * tq, tq)
        q = q_ref[pl.ds(q0, tq), :] * scale
        q2 = jnp.concatenate([jnp.where(head0, q, 0.0), jnp.where(head0, 0.0, q)], axis=0).astype(BF16)
        acc = jnp.zeros((2 * tq, LANES), F32)
        carry = jnp.zeros((2 * tq, 1), F32)
        for j in range(n_fast):
            kblock = qb - j
            k0 = pl.multiple_of(jnp.maximum(kblock, 0) * tq, tq)
            ok = before if j == 0 else kblock >= 0
            pv, carry = tile(q2, kt_ref[:, pl.ds(k0, tq)], vb_ref[pl.ds(k0, tq), :], carry, ok)
            acc = acc + pv
        return q2, acc, carry

    def far_blocks(qb, q2, acc, carry):
        def alive(st):
            kblock, _, c = st
            return (kblock >= 0) & (jnp.max(c) > SB_DEAD)

        def k_body(st):
            kblock, a, c = st
            k0 = pl.multiple_of(kblock * tq, tq)
            pv, c = tile(q2, kt_ref[:, pl.ds(k0, tq)], vb_ref[pl.ds(k0, tq), :], c, None)
            return kblock - 1, a + pv, c

        return lax.while_loop(alive, k_body, (qb - n_fast, acc, carry))[1]

    def q_body(i, _):
        started = [(i * q_group + g,) + near_blocks(i * q_group + g) for g in range(q_group)]
        for qb, q2, acc, carry in started:
            acc = far_blocks(qb, q2, acc, carry)
            q0 = pl.multiple_of(qb * tq, tq)
            o_ref[pl.ds(q0, tq), :] = jnp.where(head0, acc[:tq], acc[tq:]).astype(o_ref.dtype)
        return 0

    lax.fori_loop(0, seq // tq // q_group, q_body, 0)


def stick_breaking(u, *, tq=256, n_fast=2, q_group=8):
    b, s, _ = u.shape
    pairs = SB_HEADS // 2
    blk = lambda off: pl.BlockSpec((None, s, LANES), lambda i, p: (i, 0, off + p))
    return pl.pallas_call(
        functools.partial(_sb_kernel, seq=s, tq=tq, n_fast=n_fast, q_group=q_group),
        grid=(b, pairs),
        in_specs=[blk(0), blk(pairs), blk(2 * pairs)],
        out_specs=pl.BlockSpec((None, s, LANES), lambda i, p: (i, 0, p)),
        out_shape=jax.ShapeDtypeStruct((b, s, SB_HEADS * HEAD_DIM), BF16),
        scratch_shapes=[pltpu.VMEM((LANES, s), BF16), pltpu.VMEM((s, LANES), BF16)],
        compiler_params=_cparams("parallel", "parallel"),
        name="stick_breaking",
    )(u, u, u)


CV_HALO = 32


def _conv_kernel(val_ref, gate_ref, hval_ref, hgate_ref, w_ref, b_ref, lng_ref, lnb_ref,
                 pw_ref, pwb_ref, o_ref, g_scr, *, tm):
    first = pl.program_id(1) == 0
    g_scr[CV_HALO:, :] = val_ref[...] * jax.nn.sigmoid(gate_ref[...])
    halo = hval_ref[...] * jax.nn.sigmoid(hgate_ref[...])
    g_scr[:CV_HALO, :] = jnp.where(first, 0.0, halo)
    w = w_ref[...]
    g_ext = g_scr[...]
    acc = jnp.zeros((tm, CV_WIDTH), F32) + b_ref[...]
    rolled = {}
    for k in range(CV_KERNEL):
        lo = CV_HALO - (CV_KERNEL - 1) + k
        start = -(-lo // 8) * 8
        shift = start - lo
        if shift not in rolled:
            rolled[shift] = pltpu.roll(g_ext, shift, axis=0) if shift else g_ext
        acc = acc + rolled[shift][start:start + tm, :] * w[k:k + 1, :]
    mu = jnp.mean(acc, axis=-1, keepdims=True)
    xc = acc - mu
    var = jnp.mean(xc * xc, axis=-1, keepdims=True)
    y = xc * lax.rsqrt(var + EPS) * lng_ref[...] + lnb_ref[...]
    y = y * jax.nn.sigmoid(y)
    out = jnp.dot(y.astype(BF16), pw_ref[...], preferred_element_type=F32) + pwb_ref[...]
    o_ref[...] = out.astype(o_ref.dtype)


def conformer_conv(u, cv_w, cv_b, ln_g, ln_b, pw_w, layer, pw_b, *, tm=2048):
    b, s, _ = u.shape
    c = CV_WIDTH
    val_blk = 3 * SB_HEADS * HEAD_DIM // c
    hpt = tm // CV_HALO
    cur = lambda off: pl.BlockSpec((None, tm, c), lambda i, t: (i, t, off))
    halo = lambda off: pl.BlockSpec((None, CV_HALO, c),
                                    lambda i, t: (i, jnp.maximum(t * hpt - 1, 0), off))
    vec = pl.BlockSpec((1, c), lambda i, t: (0, 0))
    return pl.pallas_call(
        functools.partial(_conv_kernel, tm=tm),
        grid=(b, s // tm),
        in_specs=[cur(val_blk), cur(val_blk + 1), halo(val_blk), halo(val_blk + 1),
                  pl.BlockSpec((CV_KERNEL, c), lambda i, t: (0, 0)), vec, vec, vec,
                  _layer_weight(pw_w, layer), vec],
        out_specs=pl.BlockSpec((None, tm, c), lambda i, t: (i, t, 0)),
        out_shape=jax.ShapeDtypeStruct((b, s, c), BF16),
        scratch_shapes=[pltpu.VMEM((tm + CV_HALO, c), F32)],
        compiler_params=_cparams("parallel", "parallel"),
        name="conformer_conv",
    )(u, u, u, u, cv_w, cv_b.reshape(1, c), ln_g.reshape(1, c), ln_b.reshape(1, c),
      pw_w, pw_b.reshape(1, c))


def _rope_table_kernel(pos_ref, invf_ref, cos_ref, sin_ref, *, seq):
    half = HEAD_DIM // 2
    ang = invf_ref[...] * pos_ref[...].astype(F32)
    c = jnp.cos(ang)
    s = jnp.sin(ang)
    cos_t = jnp.concatenate([c, c, c, c], axis=0)
    sin_t = jnp.concatenate([-s, s, -s, s], axis=0)
    del half
    for j in range(seq // LANES):
        cos_ref[j * LANES:(j + 1) * LANES, :] = cos_t[:, j * LANES:(j + 1) * LANES].T
        sin_ref[j * LANES:(j + 1) * LANES, :] = sin_t[:, j * LANES:(j + 1) * LANES].T


def rope_tables(positions):
    b, s = positions.shape
    half = HEAD_DIM // 2
    inv_freq = ROPE_THETA ** (-jnp.arange(half, dtype=F32) / half)
    out = jax.ShapeDtypeStruct((b, s, LANES), F32)
    return pl.pallas_call(
        functools.partial(_rope_table_kernel, seq=s),
        grid=(b,),
        in_specs=[pl.BlockSpec((None, 1, s), lambda i: (i, 0, 0)),
                  pl.BlockSpec((half, 1), lambda i: (0, 0))],
        out_specs=[pl.BlockSpec((None, s, LANES), lambda i: (i, 0, 0))] * 2,
        out_shape=[out, out],
        compiler_params=_cparams("parallel"),
        name="rope_tables",
    )(positions.reshape(b, 1, s), inv_freq.reshape(half, 1))


DL_DILATIONS = (1, 4, 16)
DL_GROUP = 8
DL_PAD = DL_STEPS


def _dil_kernel(q_ref, k_ref, v_ref, o_ref, q0s, q1s, kts, vs, stq, stk, stv, accd, md, ld, acct, mt, lt,
                *, seq):
    t = DL_STEPS
    sub = seq // 4
    head0 = lax.broadcasted_iota(jnp.int32, (t, LANES), 1) < HEAD_DIM
    qi = lax.broadcasted_iota(jnp.int32, (2 * t, 2 * t), 0) % t
    kj = lax.broadcasted_iota(jnp.int32, (2 * t, 2 * t), 1)
    band = (kj >= qi) & (kj <= qi + t)
    band_first = band & (kj >= t)

    def valid_mask(n):
        return band & (kj >= jnp.where(n == 0, t, 0))

    def emit(li, dst0, q, k, v):
        q0s[li, pl.ds(dst0, t), :] = jnp.where(head0, q, 0.0).astype(BF16)
        q1s[li, pl.ds(dst0, t), :] = jnp.where(head0, 0.0, q).astype(BF16)
        kts[li, :, pl.ds(DL_PAD + dst0, t)] = k.T.astype(BF16)
        vs[li, pl.ds(DL_PAD + dst0, t), :] = v.astype(BF16)

    for li in range(len(DL_DILATIONS)):
        kts[li, :, :DL_PAD] = jnp.zeros((LANES, DL_PAD), BF16)
        vs[li, :DL_PAD, :] = jnp.zeros((DL_PAD, LANES), BF16)

    def nat_body(c, _):
        rows = pl.ds(pl.multiple_of(c * t, t), t)
        emit(0, pl.multiple_of(c * t, t), q_ref[rows, :], k_ref[rows, :], v_ref[rows, :])
        return 0

    lax.fori_loop(0, seq // t, nat_body, 0, unroll=4)

    blocks = sub // (4 * t)

    def d4_d16_body(c, _):
        res4 = c // blocks
        m0 = (c % blocks) * 4 * t
        src = pl.ds(res4 + 4 * m0, 4 * t, stride=4)
        stq[...] = q_ref[src, :]
        stk[...] = k_ref[src, :]
        stv[...] = v_ref[src, :]
        for part in range(4):
            rows = pl.ds(part * t, t)
            emit(1, pl.multiple_of(res4 * sub + m0 + part * t, t), stq[rows, :], stk[rows, :], stv[rows, :])
        for s4 in range(4):
            rows = pl.ds(s4, t, stride=4)
            dst0 = pl.multiple_of((res4 + 4 * s4) * (seq // 16) + m0 // 4, t)
            emit(2, dst0, stq[rows, :], stk[rows, :], stv[rows, :])
        return 0

    lax.fori_loop(0, 4 * blocks, d4_d16_body, 0)

    def tile(li, r0, valid, prev):
        q2 = jnp.concatenate([q0s[li, pl.ds(r0, t), :], q1s[li, pl.ds(r0, t), :]], axis=0)
        kt = kts[li, :, pl.ds(r0, 2 * t)]
        vw = vs[li, pl.ds(r0, 2 * t), :]
        s = jnp.where(valid, jnp.dot(q2, kt, preferred_element_type=F32), -jnp.inf)
        m2 = jnp.max(s, axis=1, keepdims=True)
        p = jnp.exp(s - m2)
        l2 = jnp.sum(p, axis=1, keepdims=True)
        pv = jnp.dot(p.astype(BF16), vw, preferred_element_type=F32)
        m_new = jnp.where(head0, m2[:t], m2[t:])
        l_new = jnp.where(head0, l2[:t], l2[t:])
        acc_new = jnp.where(head0, pv[:t], pv[t:])
        if prev is not None:
            m_prev, l_prev, acc_prev = prev
            m_tile = m_new
            m_new = jnp.maximum(m_prev, m_tile)
            a_prev = jnp.exp(m_prev - m_new)
            a_tile = jnp.exp(m_tile - m_new)
            l_new = a_prev * l_prev + a_tile * l_new
            acc_new = a_prev * acc_prev + a_tile * acc_new
        return m_new, l_new, acc_new

    grp = DL_GROUP
    n_trips = seq // t // grp

    grp4 = 2 * grp
    per_res4 = sub // t

    def d4_group(i, _):
        outs = []
        for g in range(grp4):
            r0 = pl.multiple_of((i * grp4 + g) * t, t)
            outs.append((r0, tile(1, r0, band_first if g % per_res4 == 0 else band, None)))
        for r0, (m_new, l_new, acc_new) in outs:
            md[pl.ds(r0, t), :] = m_new
            ld[pl.ds(r0, t), :] = l_new
            accd[pl.ds(r0, t), :] = acc_new
        return 0

    lax.fori_loop(0, seq // t // grp4, d4_group, 0)

    nb16 = seq // 16 // t

    def d16_group(i, _):
        outs = []
        for g in range(grp):
            idx = i * grp + g
            res16 = idx // nb16
            n = g % nb16
            rows = pl.ds((res16 % 4) * sub + res16 // 4 + 4 * t * n, t, stride=4)
            prev = (md[rows, :], ld[rows, :], accd[rows, :])
            r0 = pl.multiple_of(idx * t, t)
            outs.append((rows, tile(2, r0, band_first if n == 0 else band, prev)))
        for rows, (m_new, l_new, acc_new) in outs:
            md[rows, :] = m_new
            ld[rows, :] = l_new
            accd[rows, :] = acc_new
        return 0

    lax.fori_loop(0, n_trips, d16_group, 0)

    def d1_group(i, _):
        outs = []
        for g in range(grp):
            n = i * grp + g
            r0 = pl.multiple_of(n * t, t)
            for r4 in range(4):
                src = pl.ds(r4 * sub + n * (t // 4), t // 4)
                dst = pl.ds(g * t + r4, t // 4, stride=4)
                mt[dst, :] = md[src, :]
                lt[dst, :] = ld[src, :]
                acct[dst, :] = accd[src, :]
            tmp = pl.ds(g * t, t)
            _, l_new, acc_new = tile(0, r0, valid_mask(n) if g == 0 else band,
                                     (mt[tmp, :], lt[tmp, :], acct[tmp, :]))
            outs.append((r0, acc_new / l_new))
        for r0, out in outs:
            o_ref[pl.ds(r0, t), :] = out.astype(o_ref.dtype)
        return 0

    lax.fori_loop(0, n_trips, d1_group, 0)


def dilated_attention(u):
    b, s, _ = u.shape
    assert s % (16 * DL_STEPS) == 0 and (2 * DL_GROUP) % (s // 4 // DL_STEPS) == 0
    assert DL_GROUP % (s // 16 // DL_STEPS) == 0
    pairs = DL_HEADS // 2
    first = DL_FIRST_COL // LANES
    blk = lambda off: pl.BlockSpec((None, s, LANES), lambda i, p: (i, 0, off + p))
    nl = len(DL_DILATIONS)
    stat = pltpu.VMEM((s, LANES), F32)
    small = pltpu.VMEM((DL_GROUP * DL_STEPS, LANES), F32)
    stage = pltpu.VMEM((4 * DL_STEPS, LANES), F32)
    return pl.pallas_call(
        functools.partial(_dil_kernel, seq=s),
        grid=(b, pairs),
        in_specs=[blk(first), blk(first + pairs), blk(first + 2 * pairs)],
        out_specs=pl.BlockSpec((None, s, LANES), lambda i, p: (i, 0, p)),
        out_shape=jax.ShapeDtypeStruct((b, s, DL_HEADS * HEAD_DIM), BF16),
        scratch_shapes=[pltpu.VMEM((nl, s, LANES), BF16), pltpu.VMEM((nl, s, LANES), BF16),
                        pltpu.VMEM((nl, LANES, DL_PAD + s), BF16), pltpu.VMEM((nl, DL_PAD + s, LANES), BF16),
                        stage, stage, stage, stat, stat, stat, small, small, small],
        compiler_params=_cparams("parallel", "parallel"),
        name="dilated_attention",
    )(u, u, u)


def _post_mixer_kernel(a_ref, b_ref, c_ref, h_ref, wout_ref, gmix_ref, gpre_ref, wq_ref,
                       kv_ref, wo_ref, gpost_ref, o_ref, att_ref, *, d_model):
    b_row = a_ref.shape[1]
    c_row = b_row + b_ref.shape[1]
    y = jnp.dot(a_ref[...], wout_ref[:b_row, :], preferred_element_type=F32)
    y = y + jnp.dot(b_ref[...], wout_ref[b_row:c_row, :], preferred_element_type=F32)
    y = y + jnp.dot(c_ref[...], wout_ref[c_row:, :], preferred_element_type=F32)
    h = h_ref[...] + _rms(y, gmix_ref[...])

    hd = d_model // X_HEADS
    qn = _rms(h, gpre_ref[...]).astype(BF16)
    q = (jnp.dot(qn, wq_ref[...], preferred_element_type=F32) * hd ** -0.5).astype(BF16)
    for hh in range(X_HEADS):
        k = kv_ref[:, hh * hd:(hh + 1) * hd]
        v = kv_ref[:, d_model + hh * hd:d_model + (hh + 1) * hd]
        s = lax.dot_general(q[:, hh * hd:(hh + 1) * hd], k, NT_DIMS, preferred_element_type=F32)
        p = jnp.exp(s - jnp.max(s, axis=1, keepdims=True))
        den = jnp.sum(p, axis=1, keepdims=True)
        att = jnp.dot(p.astype(BF16), v, preferred_element_type=F32) / den
        att_ref[:, hh * hd:(hh + 1) * hd] = att.astype(BF16)
    y = jnp.dot(att_ref[...], wo_ref[...], preferred_element_type=F32)
    o_ref[...] = h + _rms(y, gpost_ref[...])


def post_mixer(a_out, b_out, c_out, h, w_out, g_mix_post, g_x_pre, wq, kv, wo, g_x_post, layer, *, tm=1024):
    b, s, d = h.shape
    n_mem = kv.shape[1]
    rows = lambda w: pl.BlockSpec((None, tm, w), lambda i, t: (i, t, 0))
    vec = pl.BlockSpec((1, d), lambda i, t: (0, 0))
    return pl.pallas_call(
        functools.partial(_post_mixer_kernel, d_model=d),
        grid=(b, s // tm),
        in_specs=[rows(a_out.shape[2]), rows(b_out.shape[2]), rows(c_out.shape[2]), rows(d),
                  _layer_weight(w_out, layer), vec, vec,
                  _layer_weight(wq, layer), pl.BlockSpec((None, n_mem, 2 * d), lambda i, t: (i, 0, 0)),
                  _layer_weight(wo, layer), vec],
        out_specs=rows(d),
        out_shape=jax.ShapeDtypeStruct((b, s, d), F32),
        scratch_shapes=[pltpu.VMEM((tm, d), BF16)],
        compiler_params=_cparams("parallel", "parallel"),
        name="post_mixer",
    )(a_out, b_out, c_out, h, w_out, g_mix_post.reshape(1, d), g_x_pre.reshape(1, d), wq, kv, wo,
      g_x_post.reshape(1, d))


FFN_HALO = 8


def _ffn_kernel(h_ref, halo_ref, gpre_ref, wup_ref, cw_ref, cb_ref, wdn_ref, gpost_ref, o_ref, act_ref,
                *, tm, d_ff, n_chunk):
    first = pl.program_id(1) == 0
    h = h_ref[...]
    g = gpre_ref[...]
    halo = jnp.where(first, 0.0, _rms(halo_ref[...], g))
    xn = jnp.concatenate([halo, _rms(h, g)], axis=0).astype(BF16)

    def conv(u, c0):
        y = cb_ref[:, c0:c0 + n_chunk] + u[FFN_HALO:, :] * cw_ref[FFN_KERNEL - 1:FFN_KERNEL, c0:c0 + n_chunk]
        for k in range(FFN_KERNEL - 1):
            past = pltpu.roll(u, FFN_KERNEL - 1 - k, axis=0)[FFN_HALO:, :]
            y = y + past * cw_ref[k:k + 1, c0:c0 + n_chunk]
        return y

    half = tm // 2
    for r0 in (0, half):
        xs = xn[r0:r0 + half + FFN_HALO, :]
        for c0 in range(0, d_ff, n_chunk):
            ug = jnp.dot(xs, wup_ref[:, c0:c0 + n_chunk], preferred_element_type=F32)
            uv = jnp.dot(xs, wup_ref[:, d_ff + c0:d_ff + c0 + n_chunk], preferred_element_type=F32)
            act = jax.nn.gelu(conv(ug, c0), approximate=True) * conv(uv, d_ff + c0)
            act_ref[r0:r0 + half, c0:c0 + n_chunk] = act.astype(BF16)
    for r0 in (0, half):
        y = jnp.dot(act_ref[r0:r0 + half, :], wdn_ref[...], preferred_element_type=F32)
        o_ref[r0:r0 + half, :] = h[r0:r0 + half, :] + _rms(y, gpost_ref[...])


def conv_ffn(h, g_pre, w_up, conv_w, conv_b, w_down, g_post, layer, *, tm=512, n_chunk=256):
    b, s, d = h.shape
    d_ff = w_down.shape[1]
    hpt = tm // FFN_HALO
    const = lambda shape: pl.BlockSpec(shape, lambda i, t: (0, 0))
    return pl.pallas_call(
        functools.partial(_ffn_kernel, tm=tm, d_ff=d_ff, n_chunk=n_chunk),
        grid=(b, s // tm),
        in_specs=[pl.BlockSpec((None, tm, d), lambda i, t: (i, t, 0)),
                  pl.BlockSpec((None, FFN_HALO, d), lambda i, t: (i, jnp.maximum(t * hpt - 1, 0), 0)),
                  const((1, d)), _layer_weight(w_up, layer), const((FFN_KERNEL, 2 * d_ff)),
                  const((1, 2 * d_ff)), _layer_weight(w_down, layer), const((1, d))],
        out_specs=pl.BlockSpec((None, tm, d), lambda i, t: (i, t, 0)),
        out_shape=jax.ShapeDtypeStruct((b, s, d), F32),
        scratch_shapes=[pltpu.VMEM((tm, d_ff), BF16)],
        compiler_params=_cparams("parallel", "parallel"),
        name="conv_ffn",
    )(h, h, g_pre.reshape(1, d), w_up, conv_w, conv_b.reshape(1, 2 * d_ff), w_down,
      g_post.reshape(1, d))


def kernel(x, mem, positions, mix_norm_pre, w_in, cv_w, cv_b, cv_ln_g, cv_ln_b, cv_pw_w, cv_pw_b,
           w_out, mix_norm_post, x_norm_pre, mem_norm, x_wq, x_wk, x_wv, x_wo, x_norm_post,
           ffn_norm_pre, ffn_w_up, ffn_conv_w, ffn_conv_b, ffn_w_down, ffn_norm_post):
    b, s, d = x.shape
    n_mem = mem.shape[1]
    depth = w_in.shape[0]
    m = b * s
    tm = 1024

    cos_t, sin_t = (tab.reshape(m, LANES) for tab in rope_tables(positions))
    mem2 = mem.reshape(b * n_mem, d)
    w_in, cv_pw_w, w_out, x_wq, x_wo, ffn_w_up, ffn_w_down = (
        w.astype(BF16) for w in (w_in, cv_pw_w, w_out, x_wq, x_wo, ffn_w_up, ffn_w_down))
    x_wkv = jnp.concatenate([x_wk, x_wv], axis=2).astype(BF16)
    h = x
    for l in range(depth):
        u = mixer_in_proj(h.reshape(m, d), mix_norm_pre[l], w_in, l, cos_t, sin_t, tm=tm).reshape(b, s, -1)
        a_out = stick_breaking(u)
        b_out = conformer_conv(u, cv_w[l], cv_b[l], cv_ln_g[l], cv_ln_b[l], cv_pw_w, l, cv_pw_b[l])
        c_out = dilated_attention(u)
        kv = norm_matmul(mem2, mem_norm[l], x_wkv, l, out_dtype=BF16, tm=tm).reshape(b, n_mem, 2 * d)
        h = post_mixer(a_out, b_out, c_out, h, w_out, mix_norm_post[l], x_norm_pre[l],
                       x_wq, kv, x_wo, x_norm_post[l], l)
        h = conv_ffn(h, ffn_norm_pre[l], ffn_w_up, ffn_conv_w[l], ffn_conv_b[l], ffn_w_down,
                     ffn_norm_post[l], l)
    return h
```
